```python
import math
import jax, jax.numpy as jnp
from jax import lax
import numpy as np

D_MODEL = 2048
BATCH = 1
SEQ = 8192
DEPTH = 1

D_MIX = D_MODEL
D_HYENA = D_MIX // 2
D_ATTN = D_MIX - D_HYENA
N_DIFF_HEADS = 8
DIFF_HEAD_DIM = D_ATTN // (2 * N_DIFF_HEADS)
VAL_DIM = 2 * DIFF_HEAD_DIM
D_QK = 2 * N_DIFF_HEADS * DIFF_HEAD_DIM
D_IN = 3 * D_HYENA + 2 * D_QK + N_DIFF_HEADS * VAL_DIM
ROT_DIM = DIFF_HEAD_DIM // 4
ROPE_THETA = 500000.0
Q_BLOCK = 128
HYENA_ORDER = 2
SHORT_CONV = 3
FILTER_EMB = 33
FILTER_HIDDEN = 64
N_DIRS = 2
DECAY_TARGET = 1e-2
FAST_DECAY_PCT = 0.3
SLOW_DECAY_PCT = 1.5
MIN_DECAY = math.log(DECAY_TARGET) / SLOW_DECAY_PCT
MAX_DECAY = math.log(DECAY_TARGET) / FAST_DECAY_PCT
D_FF = 5632
LN_EPS = 1e-5
ALPHA = (2.0 * DEPTH) ** 0.25
BETA = (8.0 * DEPTH) ** -0.25

kernel_name = "hybrid_hyena_diffattn_macaron_encoder"


def lambda_init_for(layer_idx):
    return 0.8 - 0.6 * math.exp(-0.3 * (layer_idx - 1))


def layer_norm(x, g, b):
    xf = x.astype(jnp.float32)
    mu = jnp.mean(xf, axis=-1, keepdims=True)
    var = jnp.mean(jnp.square(xf - mu), axis=-1, keepdims=True)
    return ((xf - mu) * lax.rsqrt(var + LN_EPS)).astype(x.dtype) * g + b


def rms_norm(x, g):
    xf = x.astype(jnp.float32)
    return (xf * lax.rsqrt(jnp.mean(jnp.square(xf), axis=-1, keepdims=True) + LN_EPS)).astype(x.dtype) * g


def swiglu(x, w_gate, w_up, w_down):
    return (jax.nn.silu(x @ w_gate) * (x @ w_up)) @ w_down


def short_conv(u, w, b):
    L = u.shape[1]
    up = jnp.pad(u, ((0, 0), (1, 1), (0, 0)))
    return up[:, :L] * w[0] + up[:, 1:L + 1] * w[1] + up[:, 2:] * w[2] + b


def hyena_filters(L, w1, b1, w2, b2, w3, b3, freq, w_out):
    t = jnp.linspace(0.0, 1.0, L, dtype=jnp.float32)[:, None]
    bands = (FILTER_EMB - 1) // 2
    w = 2.0 * math.pi * jnp.arange(L, dtype=jnp.float32)[:, None] / L
    f = jnp.linspace(1e-4, bands - 1, bands, dtype=jnp.float32)[None, :]
    z = jnp.concatenate([t, jnp.cos(f * w), -jnp.sin(f * w)], axis=-1)
    h = jnp.sin(freq * (z @ w1 + b1))
    h = jnp.sin(freq * (h @ w2 + b2))
    h = jnp.sin(freq * (h @ w3 + b3))
    h = (h @ w_out).astype(jnp.float32)
    deltas = jnp.abs(jnp.linspace(MIN_DECAY, MAX_DECAY, D_HYENA, dtype=jnp.float32))
    decay = jnp.exp(-t * deltas)
    return h.reshape(L, HYENA_ORDER, N_DIRS, D_HYENA) * decay[:, None, None, :]


def bidir_fftconv(u, h_fwd, h_bwd, bias):
    L = u.shape[1]
    k = jnp.concatenate([h_fwd, jnp.zeros_like(h_fwd[:1]), h_bwd[:0:-1]], axis=0)
    k_f = jnp.fft.rfft(k, n=2 * L, axis=0)
    u_f = jnp.fft.rfft(u.astype(jnp.float32), n=2 * L, axis=1)
    y = jnp.fft.irfft(u_f * k_f[None], n=2 * L, axis=1)[:, :L]
    return (y + u.astype(jnp.float32) * bias.astype(jnp.float32)).astype(u.dtype)


def hyena_group(p_hy, conv_w, conv_b, fw1, fb1, fw2, fb2, fw3, fb3, ffreq, fw_out, hyena_bias, hyena_norm_g):
    L = p_hy.shape[1]
    z = short_conv(p_hy, conv_w, conv_b)
    v, g1, g2 = jnp.split(z, 3, axis=-1)
    filt = hyena_filters(L, fw1, fb1, fw2, fb2, fw3, fb3, ffreq, fw_out)
    gates = (g1, g2)
    y = v
    for n in range(HYENA_ORDER):
        y = gates[n] * bidir_fftconv(y, filt[:, n, 0], filt[:, n, 1], hyena_bias[n])
    return rms_norm(y, hyena_norm_g)


def partial_rope(x, pos):
    half = ROT_DIM // 2
    inv = ROPE_THETA ** (-jnp.arange(0, ROT_DIM, 2, dtype=jnp.float32) / ROT_DIM)
    ang = pos[:, None] * inv[None, :]
    cos = jnp.concatenate([jnp.cos(ang), jnp.cos(ang)], axis=-1)[None, :, None, None, :]
    sin = jnp.concatenate([jnp.sin(ang), jnp.sin(ang)], axis=-1)[None, :, None, None, :]
    xr = x[..., :ROT_DIM].astype(jnp.float32)
    rot = jnp.concatenate([-xr[..., half:], xr[..., :half]], axis=-1)
    xr = (xr * cos + rot * sin).astype(x.dtype)
    return jnp.concatenate([xr, x[..., ROT_DIM:]], axis=-1)


def diff_attention_group(p_q, p_k, p_v, lq1, lk1, lq2, lk2, subln_g, lambda_init):
    B, L, _ = p_q.shape
    nb = L // Q_BLOCK
    pos = jnp.arange(L, dtype=jnp.float32)
    q = partial_rope(p_q.reshape(B, L, N_DIFF_HEADS, 2, DIFF_HEAD_DIM), pos) * (DIFF_HEAD_DIM ** -0.5)
    k = partial_rope(p_k.reshape(B, L, N_DIFF_HEADS, 2, DIFF_HEAD_DIM), pos)
    v = p_v.reshape(B, L, N_DIFF_HEADS, VAL_DIM)
    lam = (jnp.exp(jnp.sum(lq1.astype(jnp.float32) * lk1.astype(jnp.float32)))
           - jnp.exp(jnp.sum(lq2.astype(jnp.float32) * lk2.astype(jnp.float32))) + lambda_init)
    qb = q.reshape(B, nb, Q_BLOCK, N_DIFF_HEADS, 2, DIFF_HEAD_DIM).transpose(1, 0, 3, 4, 2, 5)
    kt = k.transpose(0, 2, 3, 1, 4)
    vt = v.transpose(0, 2, 1, 3)

    def block(q_blk):
        s = jnp.einsum('bhmqd,bhmkd->bhmqk', q_blk, kt).astype(jnp.float32)
        a = jax.nn.softmax(s, axis=-1)
        w = a[:, :, 0] - lam * a[:, :, 1]
        return jnp.einsum('bhqk,bhkv->bhqv', w.astype(vt.dtype), vt)

    o = lax.map(block, qb)
    o = o.transpose(1, 0, 3, 2, 4).reshape(B, L, N_DIFF_HEADS, VAL_DIM)
    o = rms_norm(o, subln_g) * (1.0 - lambda_init)
    return o.reshape(B, L, D_ATTN)


def hybrid_mixer(h, w_in, conv_w, conv_b, fw1, fb1, fw2, fb2, fw3, fb3, ffreq, fw_out,
                 hyena_bias, hyena_norm_g, lq1, lk1, lq2, lk2, subln_g, w_out, lambda_init):
    proj = h @ w_in
    s1 = 3 * D_HYENA
    p_hy, p_q, p_k, p_v = jnp.split(proj, [s1, s1 + D_QK, s1 + 2 * D_QK], axis=-1)
    y_hy = hyena_group(p_hy, conv_w, conv_b, fw1, fb1, fw2, fb2, fw3, fb3, ffreq, fw_out,
                       hyena_bias, hyena_norm_g)
    y_at = diff_attention_group(p_q, p_k, p_v, lq1, lk1, lq2, lk2, subln_g, lambda_init)
    return jnp.concatenate([y_hy, y_at], axis=-1) @ w_out


def setup_inputs(seed: int = 0) -> dict:
    key = jax.random.key(seed)
    ks = iter(jax.random.split(key, 40))

    def nrm(shape, scale):
        return jax.random.normal(next(ks), shape, jnp.float32) * scale

    def gain(shape):
        return 1.0 + nrm(shape, 0.05)

    n = DEPTH
    return {
        "x": nrm((BATCH, SEQ, D_MODEL), 1.0),
        "ffn1_w_gate": nrm((n, D_MODEL, D_FF), D_MODEL ** -0.5),
        "ffn1_w_up": nrm((n, D_MODEL, D_FF), D_MODEL ** -0.5),
        "ffn1_w_down": nrm((n, D_FF, D_MODEL), BETA * D_FF ** -0.5),
        "ln1_g": gain((n, D_MODEL)),
        "ln1_b": nrm((n, D_MODEL), 0.02),
        "w_in": nrm((n, D_MODEL, D_IN), D_MODEL ** -0.5),
        "hyena_conv_w": nrm((n, SHORT_CONV, 3 * D_HYENA), 0.5),
        "hyena_conv_b": nrm((n, 3 * D_HYENA), 0.02),
        "filt_w1": nrm((n, FILTER_EMB, FILTER_HIDDEN), FILTER_EMB ** -0.5),
        "filt_b1": nrm((n, FILTER_HIDDEN), 0.1),
        "filt_w2": nrm((n, FILTER_HIDDEN, FILTER_HIDDEN), FILTER_HIDDEN ** -0.5),
        "filt_b2": nrm((n, FILTER_HIDDEN), 0.1),
        "filt_w3": nrm((n, FILTER_HIDDEN, FILTER_HIDDEN), FILTER_HIDDEN ** -0.5),
        "filt_b3": nrm((n, FILTER_HIDDEN), 0.1),
        "filt_freq": gain((n, FILTER_HIDDEN)),
        "filt_w_out": nrm((n, FILTER_HIDDEN, HYENA_ORDER * N_DIRS * D_HYENA), FILTER_HIDDEN ** -0.5),
        "hyena_bias": nrm((n, HYENA_ORDER, D_HYENA), 0.5),
        "hyena_norm_g": gain((n, D_HYENA)),
        "lambda_q1": nrm((n, DIFF_HEAD_DIM), 0.1),
        "lambda_k1": nrm((n, DIFF_HEAD_DIM), 0.1),
        "lambda_q2": nrm((n, DIFF_HEAD_DIM), 0.1),
        "lambda_k2": nrm((n, DIFF_HEAD_DIM), 0.1),
        "subln_g": gain((n, VAL_DIM)),
        "w_out": nrm((n, D_MIX, D_MODEL), BETA * D_MIX ** -0.5),
        "ln2_g": gain((n, D_MODEL)),
        "ln2_b": nrm((n, D_MODEL), 0.02),
        "ffn2_w_gate": nrm((n, D_MODEL, D_FF), D_MODEL ** -0.5),
        "ffn2_w_up": nrm((n, D_MODEL, D_FF), D_MODEL ** -0.5),
        "ffn2_w_down": nrm((n, D_FF, D_MODEL), BETA * D_FF ** -0.5),
        "ln3_g": gain((n, D_MODEL)),
        "ln3_b": nrm((n, D_MODEL), 0.02),
    }


def reference(x, ffn1_w_gate, ffn1_w_up, ffn1_w_down, ln1_g, ln1_b, w_in, hyena_conv_w, hyena_conv_b,
              filt_w1, filt_b1, filt_w2, filt_b2, filt_w3, filt_b3, filt_freq, filt_w_out, hyena_bias,
              hyena_norm_g, lambda_q1, lambda_k1, lambda_q2, lambda_k2, subln_g, w_out, ln2_g, ln2_b,
              ffn2_w_gate, ffn2_w_up, ffn2_w_down, ln3_g, ln3_b):
    for i in range(DEPTH):
        lam_init = lambda_init_for(i + 1)
        x = layer_norm(ALPHA * x + 0.5 * swiglu(x, ffn1_w_gate[i], ffn1_w_up[i], ffn1_w_down[i]),
                       ln1_g[i], ln1_b[i])
        y = hybrid_mixer(x, w_in[i], hyena_conv_w[i], hyena_conv_b[i], filt_w1[i], filt_b1[i],
                         filt_w2[i], filt_b2[i], filt_w3[i], filt_b3[i], filt_freq[i], filt_w_out[i],
                         hyena_bias[i], hyena_norm_g[i], lambda_q1[i], lambda_k1[i], lambda_q2[i],
                         lambda_k2[i], subln_g[i], w_out[i], lam_init)
        x = layer_norm(ALPHA * x + y, ln2_g[i], ln2_b[i])
        x = layer_norm(ALPHA * x + 0.5 * swiglu(x, ffn2_w_gate[i], ffn2_w_up[i], ffn2_w_down[i]),
                       ln3_g[i], ln3_b[i])
    return x
```

```python
import functools
import math

import numpy as np
import jax
import jax.numpy as jnp
from jax import lax
from jax.experimental import pallas as pl
from jax.experimental.pallas import tpu as pltpu

F32 = jnp.float32
BF16 = jnp.bfloat16

D_HYENA = 1024
N_HEADS = 8
HEAD_DIM = 64
VAL_DIM = 128
D_QK = 2 * N_HEADS * HEAD_DIM
ROT_DIM = 16
ROPE_THETA = 500000.0
FILTER_EMB = 33
FILTER_HIDDEN = 64
DECAY_TARGET = 1e-2
MIN_DECAY = math.log(DECAY_TARGET) / 1.5
MAX_DECAY = math.log(DECAY_TARGET) / 0.3
LN_EPS = 1e-5
DEPTH = 1
ALPHA = (2.0 * DEPTH) ** 0.25
LAMBDA_INIT = 0.8 - 0.6 * math.exp(-0.3 * 0)

LANES = 128
FFT_MINOR = 128
VMEM_LIMIT = 56 * 1024 * 1024


def _cparams(sem):
    return pltpu.CompilerParams(dimension_semantics=sem, vmem_limit_bytes=VMEM_LIMIT)


def _layer_norm(y, g, b):
    mu = jnp.mean(y, axis=-1, keepdims=True)
    d = y - mu
    var = jnp.mean(d * d, axis=-1, keepdims=True)
    return d * lax.rsqrt(var + LN_EPS) * g + b


def _ffn_kernel(x_ref, wg_ref, wu_ref, wd_ref, g_ref, b_ref, o_ref, xb_ref, acc_ref, *, nf):
    f = pl.program_id(1)

    @pl.when(f == 0)
    def _():
        xb_ref[...] = x_ref[...].astype(BF16)
        acc_ref[...] = jnp.zeros_like(acc_ref)

    xb = xb_ref[...]
    hg = jnp.dot(xb, wg_ref[...], preferred_element_type=F32)
    hu = jnp.dot(xb, wu_ref[...], preferred_element_type=F32)
    h = hg * jax.nn.sigmoid(hg) * hu
    acc_ref[...] += jnp.dot(h.astype(BF16), wd_ref[...], preferred_element_type=F32)

    @pl.when(f == nf - 1)
    def _():
        y = ALPHA * x_ref[...] + 0.5 * acc_ref[...]
        o_ref[...] = _layer_norm(y, g_ref[...], b_ref[...])


def _ffn_ln(x, wg, wu, wd, g, b, tm=512, tf=512):
    L, D = x.shape
    F = wg.shape[1]
    tm = min(tm, L)
    nf = F // tf
    return pl.pallas_call(
        functools.partial(_ffn_kernel, nf=nf),
        out_shape=jax.ShapeDtypeStruct((L, D), F32),
        grid=(L // tm, nf),
        in_specs=[
            pl.BlockSpec((tm, D), lambda i, f: (i, 0)),
            pl.BlockSpec((D, tf), lambda i, f: (0, f)),
            pl.BlockSpec((D, tf), lambda i, f: (0, f)),
            pl.BlockSpec((tf, D), lambda i, f: (f, 0)),
            pl.BlockSpec((1, D), lambda i, f: (0, 0)),
            pl.BlockSpec((1, D), lambda i, f: (0, 0)),
        ],
        out_specs=pl.BlockSpec((tm, D), lambda i, f: (i, 0)),
        scratch_shapes=[pltpu.VMEM((tm, D), BF16), pltpu.VMEM((tm, D), F32)],
        compiler_params=_cparams(("parallel", "arbitrary")),
        name="ffn_ln",
    )(x, wg, wu, wd, g, b)


def _proj_kernel(x_ref, w_ref, o_ref, xb_ref):
    @pl.when(pl.program_id(1) == 0)
    def _():
        xb_ref[...] = x_ref[...].astype(BF16)

    o_ref[...] = jnp.dot(xb_ref[...], w_ref[...], preferred_element_type=F32).astype(o_ref.dtype)


def _proj(x, w, out_dtype, tm=1024, tn=1024):
    L, D = x.shape
    N = w.shape[1]
    tm = min(tm, L)
    return pl.pallas_call(
        _proj_kernel,
        out_shape=jax.ShapeDtypeStruct((L, N), out_dtype),
        grid=(L // tm, N // tn),
        in_specs=[pl.BlockSpec((tm, D), lambda i, j: (i, 0)),
                  pl.BlockSpec((D, tn), lambda i, j: (0, j))],
        out_specs=pl.BlockSpec((tm, tn), lambda i, j: (i, j)),
        scratch_shapes=[pltpu.VMEM((tm, D), BF16)],
        compiler_params=_cparams(("parallel", "arbitrary")),
        name="proj",
    )(x, w)


def _proj_rope_kernel(x_ref, w_ref, c_ref, s1_ref, s2_ref, o_ref, xb_ref, *, nq_blocks, tn):
    j = pl.program_id(1)

    @pl.when(j == 0)
    def _():
        xb_ref[...] = x_ref[...].astype(BF16)

    p = jnp.dot(xb_ref[...], w_ref[...], preferred_element_type=F32)
    scale = jnp.where(j < nq_blocks, HEAD_DIM ** -0.5, 1.0).astype(F32)
    c = c_ref[...] * scale
    s1 = s1_ref[...] * scale
    s2 = s2_ref[...] * scale
    for h in range(tn // LANES):
        ph = p[:, h * LANES:(h + 1) * LANES]
        r = ph * c + pltpu.roll(ph, LANES - ROT_DIM // 2, 1) * s1 + pltpu.roll(ph, ROT_DIM // 2, 1) * s2
        o_ref[:, h * LANES:(h + 1) * LANES] = r.astype(o_ref.dtype)


def _rope_tables(L):
    half = ROT_DIM // 2
    inv = ROPE_THETA ** (-np.arange(0, ROT_DIM, 2, dtype=np.float64) / ROT_DIM)
    pos = np.arange(L, dtype=np.float64)[:, None]
    d = np.arange(LANES) % HEAD_DIM
    ang = pos * inv[d % half][None, :]
    c = np.where(d[None, :] < ROT_DIM, np.cos(ang), 1.0)
    s1 = np.where(d[None, :] < half, -np.sin(ang), 0.0)
    s2 = np.where((d[None, :] >= half) & (d[None, :] < ROT_DIM), np.sin(ang), 0.0)
    return (jnp.asarray(c, F32), jnp.asarray(s1, F32), jnp.asarray(s2, F32))


def _proj_rope(x, w, tm=1024, tn=1024):
    L, D = x.shape
    N = w.shape[1]
    tm = min(tm, L)
    c, s1, s2 = _rope_tables(L)
    tab = pl.BlockSpec((tm, LANES), lambda i, j: (i, 0))
    return pl.pallas_call(
        functools.partial(_proj_rope_kernel, nq_blocks=D_QK // tn, tn=tn),
        out_shape=jax.ShapeDtypeStruct((L, N), BF16),
        grid=(L // tm, N // tn),
        in_specs=[pl.BlockSpec((tm, D), lambda i, j: (i, 0)),
                  pl.BlockSpec((D, tn), lambda i, j: (0, j)),
                  tab, tab, tab],
        out_specs=pl.BlockSpec((tm, tn), lambda i, j: (i, j)),
        scratch_shapes=[pltpu.VMEM((tm, D), BF16)],
        compiler_params=_cparams(("parallel", "arbitrary")),
        name="proj_rope",
    )(x, w, c, s1, s2)


def _attn_kernel(q_ref, k_ref, v_ref, lq1_ref, lk1_ref, lq2_ref, lk2_ref, g_ref, o_ref,
                 m_ref, l_ref, acc_ref, *, tk, nk):
    q = q_ref[...].astype(F32)
    lane = lax.broadcasted_iota(jnp.int32, q.shape, 1)
    qs = (jnp.where(lane < HEAD_DIM, q, 0.0).astype(BF16),
          jnp.where(lane >= HEAD_DIM, q, 0.0).astype(BF16))
    m_ref[...] = jnp.full(m_ref.shape, -jnp.inf, F32)
    l_ref[...] = jnp.zeros_like(l_ref)
    acc_ref[...] = jnp.zeros_like(acc_ref)

    def body(j, carry):
        r0 = pl.multiple_of(j * tk, tk)
        kb = k_ref[pl.ds(r0, tk), :]
        vb = v_ref[pl.ds(r0, tk), :]
        for mi in range(2):
            s = lax.dot_general(qs[mi], kb, (((1,), (1,)), ((), ())), preferred_element_type=F32)
            m_prev = m_ref[mi]
            m_next = jnp.maximum(m_prev, jnp.max(s, axis=1, keepdims=True))
            alpha = jnp.exp(m_prev - m_next)
            p = jnp.exp(s - pltpu.repeat(m_next, tk // LANES, axis=1))
            l_ref[mi] = alpha * l_ref[mi] + jnp.sum(p, axis=1, keepdims=True)
            m_ref[mi] = m_next
            acc_ref[mi] = acc_ref[mi] * alpha + jnp.dot(p.astype(BF16), vb, preferred_element_type=F32)
        return carry

    lax.fori_loop(0, nk, body, 0)

    lam = (jnp.exp(jnp.sum(lq1_ref[...] * lk1_ref[...], axis=-1, keepdims=True))
           - jnp.exp(jnp.sum(lq2_ref[...] * lk2_ref[...], axis=-1, keepdims=True)) + LAMBDA_INIT)
    o = acc_ref[0] / l_ref[0] - lam * (acc_ref[1] / l_ref[1])
    o = o * lax.rsqrt(jnp.mean(o * o, axis=-1, keepdims=True) + LN_EPS) * g_ref[...] * (1.0 - LAMBDA_INIT)
    o_ref[...] = o.astype(o_ref.dtype)


def _diff_attention(qk, v, lq1, lk1, lq2, lk2, subln_g, tq=512, tk=512):
    L = v.shape[0]
    tq = min(tq, L)
    tk = min(tk, L)
    vec = pl.BlockSpec((1, HEAD_DIM), lambda h, i: (0, 0))
    return pl.pallas_call(
        functools.partial(_attn_kernel, tk=tk, nk=L // tk),
        out_shape=jax.ShapeDtypeStruct((L, N_HEADS * VAL_DIM), BF16),
        grid=(N_HEADS, L // tq),
        in_specs=[pl.BlockSpec((tq, LANES), lambda h, i: (i, h)),
                  pl.BlockSpec((L, LANES), lambda h, i: (0, N_HEADS + h)),
                  pl.BlockSpec((L, VAL_DIM), lambda h, i: (0, h)),
                  vec, vec, vec, vec,
                  pl.BlockSpec((1, VAL_DIM), lambda h, i: (0, 0))],
        out_specs=pl.BlockSpec((tq, VAL_DIM), lambda h, i: (i, h)),
        scratch_shapes=[pltpu.VMEM((2, tq, LANES), F32), pltpu.VMEM((2, tq, LANES), F32),
                        pltpu.VMEM((2, tq, VAL_DIM), F32)],
        compiler_params=_cparams(("parallel", "parallel")),
        name="diff_attention",
    )(qk, qk, v, lq1, lk1, lq2, lk2, subln_g)


def _conv_kernel(x_ref, w_ref, b_ref, o_ref, *, rows, nchunks):
    w0 = w_ref[0:1, :]
    w1 = w_ref[1:2, :]
    w2 = w_ref[2:3, :]
    b = b_ref[...]
    last = nchunks * rows - 8

    def body(i, carry):
        r0 = pl.multiple_of(i * rows, rows)
        x0 = x_ref[pl.ds(r0, rows), :]
        pr = x_ref[pl.ds(pl.multiple_of(jnp.maximum(r0 - 8, 0), 8), 8), :]
        nx = x_ref[pl.ds(pl.multiple_of(jnp.minimum(r0 + rows, last), 8), 8), :]
        prev_row = jnp.where(i > 0, pr[7:8, :], 0.0)
        next_row = jnp.where(i < nchunks - 1, nx[0:1, :], 0.0)
        row = lax.broadcasted_iota(jnp.int32, x0.shape, 0)
        xm = jnp.where(row == 0, prev_row, pltpu.roll(x0, 1, 0))
        xp = jnp.where(row == rows - 1, next_row, pltpu.roll(x0, rows - 1, 0))
        o_ref[pl.ds(r0, rows), :] = xm * w0 + x0 * w1 + xp * w2 + b
        return carry

    lax.fori_loop(0, nchunks, body, 0)


def _short_conv(p, w, b, tn=256, rows=512):
    L, C = p.shape
    rows = min(rows, L)
    return pl.pallas_call(
        functools.partial(_conv_kernel, rows=rows, nchunks=L // rows),
        out_shape=jax.ShapeDtypeStruct((L, C), F32),
        grid=(C // tn,),
        in_specs=[pl.BlockSpec((L, tn), lambda j: (0, j)),
                  pl.BlockSpec((3, tn), lambda j: (0, j)),
                  pl.BlockSpec((1, tn), lambda j: (0, j))],
        out_specs=pl.BlockSpec((L, tn), lambda j: (0, j)),
        compiler_params=_cparams(("parallel",)),
        name="short_conv",
    )(p, w, b)


def _filter_kernel(z_ref, w1_ref, b1_ref, w2_ref, b2_ref, w3_ref, b3_ref, fr_ref, wo_ref,
                   dl_ref, bias_ref, o_ref, *, L, rb):
    hp = lax.Precision.HIGHEST
    fr = fr_ref[...]
    h = jnp.sin(fr * (jnp.dot(z_ref[...], w1_ref[...], precision=hp, preferred_element_type=F32) + b1_ref[...]))
    h = jnp.sin(fr * (jnp.dot(h, w2_ref[...], precision=hp, preferred_element_type=F32) + b2_ref[...]))
    h = jnp.sin(fr * (jnp.dot(h, w3_ref[...], precision=hp, preferred_element_type=F32) + b3_ref[...]))
    o = jnp.dot(h, wo_ref[0], precision=hp, preferred_element_type=F32)
    n = pl.program_id(0) * rb + lax.broadcasted_iota(jnp.int32, (rb, 1), 0)
    pos = jnp.where(n < L, n, 2 * L - n).astype(F32)
    t = pos * (1.0 / (L - 1))
    kf = o * jnp.exp(-t * dl_ref[...])
    kf = jnp.where(n == L, 0.0, kf)
    kf = kf + jnp.where(n == 0, bias_ref[...], 0.0)
    o_ref[...] = kf.astype(o_ref.dtype)


def _filter_features(L):
    n = np.arange(2 * L)
    pos = np.where(n < L, n, 2 * L - n).astype(np.float64)
    t = pos / (L - 1)
    bands = (FILTER_EMB - 1) // 2
    w = 2.0 * np.pi * pos / L
    f = np.linspace(1e-4, bands - 1, bands)
    z = np.zeros((2 * L, LANES))
    z[:, 0] = t
    z[:, 1:1 + bands] = np.cos(f[None, :] * w[:, None])
    z[:, 1 + bands:1 + 2 * bands] = -np.sin(f[None, :] * w[:, None])
    return jnp.asarray(z, F32)


def _pad_to(a, shape):
    return jnp.pad(a, [(0, s - d) for d, s in zip(a.shape, shape)])


def _hyena_filters(L, fw1, fb1, fw2, fb2, fw3, fb3, ffreq, fw_out, bias, rb=512):
    H = LANES
    z = _filter_features(L)
    w1 = _pad_to(fw1, (H, H))
    w2 = _pad_to(fw2, (H, H))
    w3 = _pad_to(fw3, (H, H))
    b1 = _pad_to(fb1[None, :], (1, H))
    b2 = _pad_to(fb2[None, :], (1, H))
    b3 = _pad_to(fb3[None, :], (1, H))
    fr = _pad_to(ffreq[None, :], (1, H))
    wo = fw_out.reshape(FILTER_HIDDEN, 2, 2, D_HYENA).transpose(2, 0, 1, 3).reshape(2, FILTER_HIDDEN, 2 * D_HYENA)
    wo = _pad_to(wo, (2, H, 2 * D_HYENA))
    deltas = np.abs(np.linspace(MIN_DECAY, MAX_DECAY, D_HYENA))
    dl = jnp.asarray(np.tile(deltas, 2)[None, :], F32)
    rb = min(rb, L)
    nb = 2 * L // rb
    full = lambda shape: pl.BlockSpec(shape, lambda i: (0,) * len(shape))
    return pl.pallas_call(
        functools.partial(_filter_kernel, L=L, rb=rb),
        out_shape=jax.ShapeDtypeStruct((2 * L, 2 * D_HYENA), BF16),
        grid=(nb,),
        in_specs=[pl.BlockSpec((rb, H), lambda i: (i, 0)),
                  full((H, H)), full((1, H)), full((H, H)), full((1, H)), full((H, H)), full((1, H)),
                  full((1, H)),
                  pl.BlockSpec((1, H, 2 * D_HYENA), lambda i: (i // (nb // 2), 0, 0)),
                  full((1, 2 * D_HYENA)), full((1, 2 * D_HYENA))],
        out_specs=pl.BlockSpec((rb, 2 * D_HYENA), lambda i: (i, 0)),
        compiler_params=_cparams(("parallel",)),
        name="hyena_filters",
    )(z, w1, b1, w2, b2, w3, b3, fr, wo, dl, bias.reshape(1, 2 * D_HYENA))


def _dft_constants(L):
    N = 2 * L
    N1 = N // FFT_MINOR
    nh = N1 // 2
    k1h = N1 // 2 + 1
    k1p = -(-k1h // 8) * 8
    k1 = np.arange(k1h)[:, None]

    def stage_a(nn):
        th = 2 * np.pi * np.arange(nn)[None, :] * k1 / N1
        fa = np.zeros((2 * k1p, nn))
        fa[:k1h] = np.cos(th)
        fa[k1p:k1p + k1h] = -np.sin(th)
        return fa

    n2 = np.arange(FFT_MINOR)
    m1 = np.zeros((k1h, 2 * FFT_MINOR, 2 * FFT_MINOR))
    m2 = np.zeros_like(m1)
    for a in range(k1h):
        ph = -2 * np.pi * (n2[None, :] * a / N + n2[None, :] * n2[:, None] / FFT_MINOR)
        cr, ci = np.cos(ph), np.sin(ph)
        m1[a] = np.block([[cr, -ci], [ci, cr]])
        m2[a] = np.block([[cr.T, ci.T], [-ci.T, cr.T]])
    kk = np.arange(k1h)[None, :]
    cw = np.where((kk == 0) | (kk == N1 // 2), 1.0, 2.0) / N
    th = 2 * np.pi * np.arange(nh)[:, None] * kk / N1
    g = np.zeros((nh, 2 * k1p))
    g[:, :k1h] = cw * np.cos(th)
    g[:, k1p:k1p + k1h] = -cw * np.sin(th)
    as_bf16 = lambda a: jnp.asarray(a, F32).astype(BF16)
    return dict(nh=nh, n1=N1, k1h=k1h, k1p=k1p, fa_data=as_bf16(stage_a(nh)), fa_filt=as_bf16(stage_a(N1)),
                m1=as_bf16(m1), m2=as_bf16(m2), g=as_bf16(g))


def _lmat_kernel(f_ref, x_ref, o_ref):
    o_ref[...] = jnp.dot(f_ref[...], x_ref[...].astype(BF16), preferred_element_type=F32).astype(o_ref.dtype)


def _lmat(fmat, x2d, tn=8192):
    M, K = fmat.shape
    ncols = x2d.shape[1]
    tn = min(tn, ncols)
    return pl.pallas_call(
        _lmat_kernel,
        out_shape=jax.ShapeDtypeStruct((M, ncols), BF16),
        grid=(ncols // tn,),
        in_specs=[pl.BlockSpec((M, K), lambda j: (0, 0)),
                  pl.BlockSpec((K, tn), lambda j: (0, j))],
        out_specs=pl.BlockSpec((M, tn), lambda j: (0, j)),
        compiler_params=_cparams(("parallel",)),
        name="dft_stage_a",
    )(fmat, x2d)


def _lmat_gate_kernel(g_ref, z_ref, gate_ref, o_ref):
    o_ref[...] = jnp.dot(g_ref[...], z_ref[...], preferred_element_type=F32) * gate_ref[...]


def _lmat_gate(gmat, z2d, gate2d, tn=8192):
    M, K = gmat.shape
    ncols = z2d.shape[1]
    tn = min(tn, ncols)
    return pl.pallas_call(
        _lmat_gate_kernel,
        out_shape=jax.ShapeDtypeStruct((M, ncols), F32),
        grid=(ncols // tn,),
        in_specs=[pl.BlockSpec((M, K), lambda j: (0, 0)),
                  pl.BlockSpec((K, tn), lambda j: (0, j)),
                  pl.BlockSpec((M, tn), lambda j: (0, j))],
        out_specs=pl.BlockSpec((M, tn), lambda j: (0, j)),
        compiler_params=_cparams(("parallel",)),
        name="dft_stage_a_inv",
    )(gmat, z2d, gate2d)


def _filter_spectrum_kernel(m1_ref, a_ref, o_ref):
    x = jnp.concatenate([a_ref[0, 0], a_ref[1, 0]], axis=0)
    o_ref[0] = jnp.dot(m1_ref[0], x, preferred_element_type=F32).astype(o_ref.dtype)


def _filter_spectrum(m1, a4, k1h):
    _, k1p, n2, C = a4.shape
    return pl.pallas_call(
        _filter_spectrum_kernel,
        out_shape=jax.ShapeDtypeStruct((k1p, 2 * n2, C), BF16),
        grid=(k1p,),
        in_specs=[pl.BlockSpec((1, 2 * n2, 2 * n2), lambda a: (jnp.minimum(a, k1h - 1), 0, 0)),
                  pl.BlockSpec((2, 1, n2, C), lambda a: (0, a, 0, 0))],
        out_specs=pl.BlockSpec((1, 2 * n2, C), lambda a: (a, 0, 0)),
        compiler_params=_cparams(("parallel",)),
        name="filter_spectrum",
    )(m1, a4)


def _middle_kernel(m1_ref, m2_ref, a_ref, kf_ref, o_ref):
    n2 = a_ref.shape[2]
    x = jnp.concatenate([a_ref[0, 0], a_ref[1, 0]], axis=0)
    s = jnp.dot(m1_ref[0], x, preferred_element_type=F32)
    sr, si = s[:n2], s[n2:]
    kf = kf_ref[0].astype(F32)
    kr, ki = kf[:n2], kf[n2:]
    y = jnp.concatenate([sr * kr - si * ki, sr * ki + si * kr], axis=0).astype(BF16)
    z = jnp.dot(m2_ref[0], y, preferred_element_type=F32).astype(o_ref.dtype)
    o_ref[0, 0] = z[:n2]
    o_ref[1, 0] = z[n2:]


def _middle(m1, m2, a4, kf, order, k1h):
    _, k1p, n2, C = a4.shape
    mat = pl.BlockSpec((1, 2 * n2, 2 * n2), lambda a: (jnp.minimum(a, k1h - 1), 0, 0))
    return pl.pallas_call(
        _middle_kernel,
        out_shape=jax.ShapeDtypeStruct((2, k1p, n2, C), BF16),
        grid=(k1p,),
        in_specs=[mat, mat,
                  pl.BlockSpec((2, 1, n2, C), lambda a: (0, a, 0, 0)),
                  pl.BlockSpec((1, 2 * n2, C), lambda a: (a, 0, order))],
        out_specs=pl.BlockSpec((2, 1, n2, C), lambda a: (0, a, 0, 0)),
        compiler_params=_cparams(("parallel",)),
        name="dft_middle",
    )(m1, m2, a4, kf)


def _hyena_long_convs(v, g1, g2, kfilt, L):
    C = v.shape[1]
    cs = _dft_constants(L)
    nh, n1, k1h, k1p = cs["nh"], cs["n1"], cs["k1h"], cs["k1p"]
    fa = _lmat(cs["fa_filt"], kfilt.reshape(n1, FFT_MINOR * 2 * C))
    kspec = _filter_spectrum(cs["m1"], fa.reshape(2, k1p, FFT_MINOR, 2 * C), k1h)
    y = v
    for order, gate in enumerate((g1, g2)):
        a = _lmat(cs["fa_data"], y.reshape(nh, FFT_MINOR * C))
        z = _middle(cs["m1"], cs["m2"], a.reshape(2, k1p, FFT_MINOR, C), kspec, order, k1h)
        y2d = _lmat_gate(cs["g"], z.reshape(2 * k1p, FFT_MINOR * C), gate.reshape(nh, FFT_MINOR * C))
        y = y2d.reshape(L, C)
    return y


def _out_kernel(yh_ref, ya_ref, wh_ref, wa_ref, ng_ref, x_ref, g_ref, b_ref, o_ref):
    yh = yh_ref[...]
    yh = yh * lax.rsqrt(jnp.mean(yh * yh, axis=-1, keepdims=True) + LN_EPS) * ng_ref[...]
    acc = jnp.dot(yh.astype(BF16), wh_ref[...], preferred_element_type=F32)
    acc = acc + jnp.dot(ya_ref[...], wa_ref[...], preferred_element_type=F32)
    o_ref[...] = _layer_norm(ALPHA * x_ref[...] + acc, g_ref[...], b_ref[...])


def _out_ln(y_hy, y_at, w_out, norm_g, x1, g, b, tm=512):
    L, D = x1.shape
    ch, ca = y_hy.shape[1], y_at.shape[1]
    tm = min(tm, L)
    row = lambda c: pl.BlockSpec((tm, c), lambda i: (i, 0))
    full = lambda r, c: pl.BlockSpec((r, c), lambda i: (0, 0))
    return pl.pallas_call(
        _out_kernel,
        out_shape=jax.ShapeDtypeStruct((L, D), F32),
        grid=(L // tm,),
        in_specs=[row(ch), row(ca),
                  pl.BlockSpec((ch, D), lambda i: (0, 0)),
                  pl.BlockSpec((ca, D), lambda i: (1, 0)),
                  full(1, ch), row(D), full(1, D), full(1, D)],
        out_specs=row(D),
        compiler_params=_cparams(("parallel",)),
        name="out_ln",
    )(y_hy, y_at, w_out, w_out, norm_g, x1, g, b)


def kernel(x, ffn1_w_gate, ffn1_w_up, ffn1_w_down, ln1_g, ln1_b, w_in, hyena_conv_w, hyena_conv_b, filt_w1, filt_b1, filt_w2, filt_b2, filt_w3, filt_b3, filt_freq, filt_w_out, hyena_bias, hyena_norm_g, lambda_q1, lambda_k1, lambda_q2, lambda_k2, subln_g, w_out, ln2_g, ln2_b, ffn2_w_gate, ffn2_w_up, ffn2_w_down, ln3_g, ln3_b):
    assert x.shape[0] == 1 and ffn1_w_gate.shape[0] == DEPTH
    L = x.shape[1]
    bf = lambda a: a.astype(BF16)
    row = lambda a: a.reshape(1, -1)
    h = x[0]
    s_hy = 3 * D_HYENA
    for i in range(DEPTH):
        h = _ffn_ln(h, bf(ffn1_w_gate[i]), bf(ffn1_w_up[i]), bf(ffn1_w_down[i]), row(ln1_g[i]), row(ln1_b[i]))
        w_in_b = bf(w_in[i])
        p_hy = _proj(h, w_in_b[:, :s_hy], F32)
        qk = _proj_rope(h, w_in_b[:, s_hy:s_hy + 2 * D_QK])
        v_at = _proj(h, w_in_b[:, s_hy + 2 * D_QK:], BF16)
        z = _short_conv(p_hy, hyena_conv_w[i], row(hyena_conv_b[i]))
        kfilt = _hyena_filters(L, filt_w1[i], filt_b1[i], filt_w2[i], filt_b2[i], filt_w3[i], filt_b3[i],
                               filt_freq[i], filt_w_out[i], hyena_bias[i])
        y_hy = _hyena_long_convs(z[:, :D_HYENA], z[:, D_HYENA:2 * D_HYENA], z[:, 2 * D_HYENA:], kfilt, L)
        y_at = _diff_attention(qk, v_at, row(lambda_q1[i]), row(lambda_k1[i]), row(lambda_q2[i]),
                               row(lambda_k2[i]), row(subln_g[i]))
        h = _out_ln(y_hy, y_at, bf(w_out[i]), row(hyena_norm_g[i]), h, row(ln2_g[i]), row(ln2_b[i]))
        h = _ffn_ln(h, bf(ffn2_w_gate[i]), bf(ffn2_w_up[i]), bf(ffn2_w_down[i]), row(ln3_g[i]), row(ln3_b[i]))
    return h[None]
```

```python
import functools
import math

import numpy as np
import jax
import jax.numpy as jnp
from jax import lax
from jax.experimental import pallas as pl
from jax.experimental.pallas import tpu as pltpu

F32 = jnp.float32
BF16 = jnp.bfloat16

D_HYENA = 1024
N_HEADS = 8
HEAD_DIM = 64
VAL_DIM = 128
D_QK = 2 * N_HEADS * HEAD_DIM
ROT_DIM = 16
ROPE_THETA = 500000.0
FILTER_EMB = 33
FILTER_HIDDEN = 64
DECAY_TARGET = 1e-2
MIN_DECAY = math.log(DECAY_TARGET) / 1.5
MAX_DECAY = math.log(DECAY_TARGET) / 0.3
LN_EPS = 1e-5
DEPTH = 1
ALPHA = (2.0 * DEPTH) ** 0.25
LAMBDA_INIT = 0.8 - 0.6 * math.exp(-0.3 * 0)
LOG2E = math.log2(math.e)

LANES = 128
FFT_MINOR = 128
VMEM_LIMIT = 56 * 1024 * 1024
ATTN_TQ = 512
ATTN_TK = 512


def _cparams(sem):
    return pltpu.CompilerParams(dimension_semantics=sem, vmem_limit_bytes=VMEM_LIMIT)


def _layer_norm(y, g, b):
    mu = jnp.mean(y, axis=-1, keepdims=True)
    d = y - mu
    var = jnp.mean(d * d, axis=-1, keepdims=True)
    return d * lax.rsqrt(var + LN_EPS) * g + b


def _ffn_kernel(x_ref, wg_ref, wu_ref, wd_ref, g_ref, b_ref, o_ref, xb_ref, acc_ref, *, nf):
    f = pl.program_id(1)

    @pl.when(f == 0)
    def _():
        xb_ref[...] = x_ref[...].astype(BF16)
        acc_ref[...] = jnp.zeros_like(acc_ref)

    xb = xb_ref[...]
    hg = jnp.dot(xb, wg_ref[...], preferred_element_type=F32)
    hu = jnp.dot(xb, wu_ref[...], preferred_element_type=F32)
    h = hg * jax.nn.sigmoid(hg) * hu
    acc_ref[...] += jnp.dot(h.astype(BF16), wd_ref[...], preferred_element_type=F32)

    @pl.when(f == nf - 1)
    def _():
        y = ALPHA * x_ref[...] + 0.5 * acc_ref[...]
        o_ref[...] = _layer_norm(y, g_ref[...], b_ref[...])


def _ffn_ln(x, wg, wu, wd, g, b, tm=512, tf=512):
    L, D = x.shape
    F = wg.shape[1]
    tm = min(tm, L)
    nf = F // tf
    return pl.pallas_call(
        functools.partial(_ffn_kernel, nf=nf),
        out_shape=jax.ShapeDtypeStruct((L, D), F32),
        grid=(L // tm, nf),
        in_specs=[
            pl.BlockSpec((tm, D), lambda i, f: (i, 0)),
            pl.BlockSpec((D, tf), lambda i, f: (0, f)),
            pl.BlockSpec((D, tf), lambda i, f: (0, f)),
            pl.BlockSpec((tf, D), lambda i, f: (f, 0)),
            pl.BlockSpec((1, D), lambda i, f: (0, 0)),
            pl.BlockSpec((1, D), lambda i, f: (0, 0)),
        ],
        out_specs=pl.BlockSpec((tm, D), lambda i, f: (i, 0)),
        scratch_shapes=[pltpu.VMEM((tm, D), BF16), pltpu.VMEM((tm, D), F32)],
        compiler_params=_cparams(("parallel", "arbitrary")),
        name="ffn_ln",
    )(x, wg, wu, wd, g, b)


def _proj_kernel(x_ref, w_ref, o_ref, xb_ref):
    @pl.when(pl.program_id(1) == 0)
    def _():
        xb_ref[...] = x_ref[...].astype(BF16)

    o_ref[...] = jnp.dot(xb_ref[...], w_ref[...], preferred_element_type=F32).astype(o_ref.dtype)


def _proj(x, w, out_dtype, tm=1024, tn=1024):
    L, D = x.shape
    N = w.shape[1]
    tm = min(tm, L)
    return pl.pallas_call(
        _proj_kernel,
        out_shape=jax.ShapeDtypeStruct((L, N), out_dtype),
        grid=(L // tm, N // tn),
        in_specs=[pl.BlockSpec((tm, D), lambda i, j: (i, 0)),
                  pl.BlockSpec((D, tn), lambda i, j: (0, j))],
        out_specs=pl.BlockSpec((tm, tn), lambda i, j: (i, j)),
        scratch_shapes=[pltpu.VMEM((tm, D), BF16)],
        compiler_params=_cparams(("parallel", "arbitrary")),
        name="proj",
    )(x, w)


def _rope_tables(L, scale):
    half = ROT_DIM // 2
    inv = ROPE_THETA ** (-np.arange(0, ROT_DIM, 2, dtype=np.float64) / ROT_DIM)
    pos = np.arange(L, dtype=np.float64)[:, None]
    d = np.arange(LANES) % HEAD_DIM
    ang = pos * inv[d % half][None, :]
    c = np.where(d[None, :] < ROT_DIM, np.cos(ang), 1.0) * scale
    s1 = np.where(d[None, :] < half, -np.sin(ang), 0.0) * scale
    s2 = np.where((d[None, :] >= half) & (d[None, :] < ROT_DIM), np.sin(ang), 0.0) * scale
    return c, s1, s2


def _proj_rope_kernel(x_ref, w_ref, c_ref, s1_ref, s2_ref, o_ref, xb_ref, *, tn):
    @pl.when(pl.program_id(1) == 0)
    def _():
        xb_ref[...] = x_ref[...].astype(BF16)

    p = jnp.dot(xb_ref[...], w_ref[...], preferred_element_type=F32)
    c, s1, s2 = c_ref[...], s1_ref[...], s2_ref[...]
    shift = ROT_DIM // 2
    for h in range(tn // LANES):
        ph = p[:, h * LANES:(h + 1) * LANES]
        r = ph * c + pltpu.roll(ph, LANES - shift, 1) * s1 + pltpu.roll(ph, shift, 1) * s2
        o_ref[:, h * LANES:(h + 1) * LANES] = r.astype(o_ref.dtype)


def _proj_rope(x, w, tm=1024, tn=1024):
    L, D = x.shape
    N = w.shape[1]
    tm = min(tm, L)
    tabs = [jnp.asarray(t, F32) for t in _rope_tables(L, 1.0)]
    tab = pl.BlockSpec((tm, LANES), lambda i, j: (i, 0))
    return pl.pallas_call(
        functools.partial(_proj_rope_kernel, tn=tn),
        out_shape=jax.ShapeDtypeStruct((L, N), BF16),
        grid=(L // tm, N // tn),
        in_specs=[pl.BlockSpec((tm, D), lambda i, j: (i, 0)),
                  pl.BlockSpec((D, tn), lambda i, j: (0, j)),
                  tab, tab, tab],
        out_specs=pl.BlockSpec((tm, tn), lambda i, j: (i, j)),
        scratch_shapes=[pltpu.VMEM((tm, D), BF16)],
        compiler_params=_cparams(("parallel", "arbitrary")),
        name="proj_rope",
    )(x, w, *tabs)


def _projT_kernel(x_ref, wt_ref, *rest, rope, tn, tk):
    if rope:
        c_ref, s1_ref, s2_ref, o_ref, xb_ref = rest
    else:
        o_ref, xb_ref = rest

    @pl.when(pl.program_id(1) == 0)
    def _():
        xb_ref[...] = x_ref[...].astype(BF16)

    pt = lax.dot_general(wt_ref[...], xb_ref[...], (((1,), (1,)), ((), ())), preferred_element_type=F32)
    if rope:
        c, s1, s2 = c_ref[...], s1_ref[...], s2_ref[...]
        shift = ROT_DIM // 2
        for h in range(tn // LANES):
            ph = pt[h * LANES:(h + 1) * LANES, :]
            r = ph * c + pltpu.roll(ph, LANES - shift, 0) * s1 + pltpu.roll(ph, shift, 0) * s2
            o_ref[h * LANES:(h + 1) * LANES, :] = r.astype(o_ref.dtype)
    else:
        for s in range(pt.shape[1] // tk):
            o_ref[s] = pt[:, s * tk:(s + 1) * tk].astype(o_ref.dtype)


def _proj_q_t(x, wt, tm=1024, tn=512):
    L, D = x.shape
    N = wt.shape[0]
    tm = min(tm, L)
    tabs = [jnp.asarray(t.T, F32) for t in _rope_tables(L, HEAD_DIM ** -0.5 * LOG2E)]
    tab = pl.BlockSpec((LANES, tm), lambda i, j: (0, i))
    return pl.pallas_call(
        functools.partial(_projT_kernel, rope=True, tn=tn, tk=None),
        out_shape=jax.ShapeDtypeStruct((N, L), BF16),
        grid=(L // tm, N // tn),
        in_specs=[pl.BlockSpec((tm, D), lambda i, j: (i, 0)),
                  pl.BlockSpec((tn, D), lambda i, j: (j, 0)),
                  tab, tab, tab],
        out_specs=pl.BlockSpec((tn, tm), lambda i, j: (j, i)),
        scratch_shapes=[pltpu.VMEM((tm, D), BF16)],
        compiler_params=_cparams(("parallel", "arbitrary")),
        name="proj_q_t",
    )(x, wt, *tabs)


def _proj_v_t(x, wt, tk, tm=1024, tn=512):
    L, D = x.shape
    N = wt.shape[0]
    tm = min(tm, L)
    tk = min(tk, L)
    per = tm // tk
    return pl.pallas_call(
        functools.partial(_projT_kernel, rope=False, tn=tn, tk=tk),
        out_shape=jax.ShapeDtypeStruct((L // tk, N, tk), BF16),
        grid=(L // tm, N // tn),
        in_specs=[pl.BlockSpec((tm, D), lambda i, j: (i, 0)),
                  pl.BlockSpec((tn, D), lambda i, j: (j, 0))],
        out_specs=pl.BlockSpec((per, tn, tk), lambda i, j: (i, j, 0)),
        scratch_shapes=[pltpu.VMEM((tm, D), BF16)],
        compiler_params=_cparams(("parallel", "arbitrary")),
        name="proj_v_t",
    )(x, wt)


def _attn_kernel(qt_ref, k_ref, vt_ref, lq1_ref, lk1_ref, lq2_ref, lk2_ref, g_ref, o_ref, acc_ref, sa_ref, sb_ref,
                 *, tk, nk):
    qt = qt_ref[...].astype(F32)
    tq = qt.shape[1]
    row = lax.broadcasted_iota(jnp.int32, qt.shape, 0)
    qs = (jnp.where(row < HEAD_DIM, qt, 0.0).astype(BF16),
          jnp.where(row >= HEAD_DIM, qt, 0.0).astype(BF16))
    acc_ref[...] = jnp.zeros_like(acc_ref)
    neg = jnp.full((1, tq), -jnp.inf, F32)
    zero = jnp.zeros((1, tq), F32)

    def scores(j, s_ref):
        kb = k_ref[pl.ds(pl.multiple_of(j * tk, tk), tk), :]
        mx = []
        for mi in range(2):
            st = jnp.dot(kb, qs[mi], preferred_element_type=F32)
            s_ref[mi] = st
            mx.append(jnp.max(st, axis=0, keepdims=True))
        return tuple(mx)

    def consume(j, s_ref, mx, ms, ls):
        vb = vt_ref[j]
        new_m, new_l = [], []
        for mi in range(2):
            m_next = jnp.maximum(ms[mi], mx[mi])
            alpha = jnp.exp2(ms[mi] - m_next)
            pt = jnp.exp2(s_ref[mi] - m_next)
            new_l.append(alpha * ls[mi] + jnp.sum(pt, axis=0, keepdims=True))
            new_m.append(m_next)
            acc_ref[mi] = acc_ref[mi] * alpha + jnp.dot(vb, pt.astype(BF16), preferred_element_type=F32)
        return tuple(new_m), tuple(new_l)

    def body(jj, carry):
        ms, ls, mxa = carry
        j = 2 * jj
        mxb = scores(j + 1, sb_ref)
        ms, ls = consume(j, sa_ref, mxa, ms, ls)
        mxa = scores(jnp.minimum(j + 2, nk - 1), sa_ref)
        ms, ls = consume(j + 1, sb_ref, mxb, ms, ls)
        return ms, ls, mxa

    _, ls, _ = lax.fori_loop(0, nk // 2, body, ((neg, neg), (zero, zero), scores(0, sa_ref)))

    lam = (jnp.exp(jnp.sum(lq1_ref[...] * lk1_ref[...], axis=-1, keepdims=True))
           - jnp.exp(jnp.sum(lq2_ref[...] * lk2_ref[...], axis=-1, keepdims=True)) + LAMBDA_INIT)
    ot = acc_ref[0] / ls[0] - lam * (acc_ref[1] / ls[1])
    ot = ot * lax.rsqrt(jnp.mean(ot * ot, axis=0, keepdims=True) + LN_EPS)
    o_ref[...] = (ot.T * (g_ref[...] * (1.0 - LAMBDA_INIT))).astype(o_ref.dtype)


def _diff_attention(qt, k, vt, lq1, lk1, lq2, lk2, subln_g, tq=ATTN_TQ):
    L = k.shape[0]
    nk, _, tk = vt.shape
    assert nk % 2 == 0, "the kv loop handles chunks in pairs"
    tq = min(tq, L)
    vec = pl.BlockSpec((1, HEAD_DIM), lambda h, i: (0, 0))
    return pl.pallas_call(
        functools.partial(_attn_kernel, tk=tk, nk=nk),
        out_shape=jax.ShapeDtypeStruct((L, N_HEADS * VAL_DIM), BF16),
        grid=(N_HEADS, L // tq),
        in_specs=[pl.BlockSpec((LANES, tq), lambda h, i: (h, i)),
                  pl.BlockSpec((L, LANES), lambda h, i: (0, h)),
                  pl.BlockSpec((nk, VAL_DIM, tk), lambda h, i: (0, h, 0)),
                  vec, vec, vec, vec,
                  pl.BlockSpec((1, VAL_DIM), lambda h, i: (0, 0))],
        out_specs=pl.BlockSpec((tq, VAL_DIM), lambda h, i: (i, h)),
        scratch_shapes=[pltpu.VMEM((2, VAL_DIM, tq), F32),
                        pltpu.VMEM((2, tk, tq), F32), pltpu.VMEM((2, tk, tq), F32)],
        compiler_params=_cparams(("parallel", "parallel")),
        name="diff_attention",
    )(qt, k, vt, lq1, lk1, lq2, lk2, subln_g)


def _conv_kernel(x_ref, w_ref, b_ref, o_ref, *, rows, nchunks):
    w0 = w_ref[0:1, :]
    w1 = w_ref[1:2, :]
    w2 = w_ref[2:3, :]
    b = b_ref[...]
    last = nchunks * rows - 8

    def body(i, carry):
        r0 = pl.multiple_of(i * rows, rows)
        x0 = x_ref[pl.ds(r0, rows), :]
        pr = x_ref[pl.ds(pl.multiple_of(jnp.maximum(r0 - 8, 0), 8), 8), :]
        nx = x_ref[pl.ds(pl.multiple_of(jnp.minimum(r0 + rows, last), 8), 8), :]
        prev_row = jnp.where(i > 0, pr[7:8, :], 0.0)
        next_row = jnp.where(i < nchunks - 1, nx[0:1, :], 0.0)
        row = lax.broadcasted_iota(jnp.int32, x0.shape, 0)
        xm = jnp.where(row == 0, prev_row, pltpu.roll(x0, 1, 0))
        xp = jnp.where(row == rows - 1, next_row, pltpu.roll(x0, rows - 1, 0))
        o_ref[pl.ds(r0, rows), :] = xm * w0 + x0 * w1 + xp * w2 + b
        return carry

    lax.fori_loop(0, nchunks, body, 0)


def _short_conv(p, w, b, tn=256, rows=512):
    L, C = p.shape
    rows = min(rows, L)
    return pl.pallas_call(
        functools.partial(_conv_kernel, rows=rows, nchunks=L // rows),
        out_shape=jax.ShapeDtypeStruct((L, C), F32),
        grid=(C // tn,),
        in_specs=[pl.BlockSpec((L, tn), lambda j: (0, j)),
                  pl.BlockSpec((3, tn), lambda j: (0, j)),
                  pl.BlockSpec((1, tn), lambda j: (0, j))],
        out_specs=pl.BlockSpec((L, tn), lambda j: (0, j)),
        compiler_params=_cparams(("parallel",)),
        name="short_conv",
    )(p, w, b)


def _split_bf16(a):
    hi = a.astype(BF16)
    return hi, (a - hi.astype(F32)).astype(BF16)


def _filter_kernel(z_ref, w1_ref, b1_ref, w2_ref, b2_ref, w3_ref, b3_ref, fr_ref, wfh_ref, wfl_ref,
                   wbh_ref, wbl_ref, dl_ref, bias_ref, o_ref, *, L, rb):
    hp = lax.Precision.HIGHEST
    fr = fr_ref[...]
    h = jnp.sin(fr * (jnp.dot(z_ref[...], w1_ref[...], precision=hp, preferred_element_type=F32) + b1_ref[...]))
    h = jnp.sin(fr * (jnp.dot(h, w2_ref[...], precision=hp, preferred_element_type=F32) + b2_ref[...]))
    h = jnp.sin(fr * (jnp.dot(h, w3_ref[...], precision=hp, preferred_element_type=F32) + b3_ref[...]))
    hh, hl = _split_bf16(h)

    def out_proj(wh_ref, wl_ref):
        wh = wh_ref[...]
        return (jnp.dot(hh, wh, preferred_element_type=F32) + jnp.dot(hl, wh, preferred_element_type=F32)
                + jnp.dot(hh, wl_ref[...], preferred_element_type=F32))

    n = pl.program_id(0) * rb + lax.broadcasted_iota(jnp.int32, (rb, 1), 0)
    step = 1.0 / (L - 1)
    dl = dl_ref[...]
    kf = out_proj(wfh_ref, wfl_ref) * jnp.exp(-(n.astype(F32) * step) * dl)
    kf = kf + jnp.where(n == 0, bias_ref[...], 0.0)
    o_ref[0] = kf.astype(o_ref.dtype)
    kb = out_proj(wbh_ref, wbl_ref) * jnp.exp(-((L - n).astype(F32) * step) * dl)
    o_ref[1] = jnp.where(n == 0, 0.0, kb).astype(o_ref.dtype)


def _filter_features(L):
    half = LANES // 2
    bands = (FILTER_EMB - 1) // 2
    f = np.linspace(1e-4, bands - 1, bands)
    z = np.zeros((L, LANES))
    n = np.arange(L, dtype=np.float64)
    for off, pos in ((0, n), (half, L - n)):
        w = 2.0 * np.pi * pos / L
        z[:, off] = pos / (L - 1)
        z[:, off + 1:off + 1 + bands] = np.cos(f[None, :] * w[:, None])
        z[:, off + 1 + bands:off + 1 + 2 * bands] = -np.sin(f[None, :] * w[:, None])
    return jnp.asarray(z, F32)


def _pad_to(a, shape):
    return jnp.pad(a, [(0, s - d) for d, s in zip(a.shape, shape)])


def _block_diag2(a):
    half = LANES // 2
    p = _pad_to(a, (half, half))
    z = jnp.zeros_like(p)
    return jnp.concatenate([jnp.concatenate([p, z], axis=1), jnp.concatenate([z, p], axis=1)], axis=0)


def _hyena_filters(L, fw1, fb1, fw2, fb2, fw3, fb3, ffreq, fw_out, bias, rb=512):
    H = LANES
    half = H // 2
    C2 = 2 * D_HYENA
    dup = lambda v: jnp.tile(_pad_to(v[None, :], (1, half)), (1, 2))
    wo = fw_out.reshape(FILTER_HIDDEN, 2, 2, D_HYENA)
    wf = _pad_to(wo[:, :, 0, :].reshape(FILTER_HIDDEN, C2), (H, C2))
    wb = jnp.concatenate([jnp.zeros((half, C2), F32), _pad_to(wo[:, :, 1, :].reshape(FILTER_HIDDEN, C2), (half, C2))])
    deltas = np.abs(np.linspace(MIN_DECAY, MAX_DECAY, D_HYENA))
    dl = jnp.asarray(np.tile(deltas, 2)[None, :], F32)
    rb = min(rb, L)
    full = lambda shape: pl.BlockSpec(shape, lambda i: (0,) * len(shape))
    out = pl.pallas_call(
        functools.partial(_filter_kernel, L=L, rb=rb),
        out_shape=jax.ShapeDtypeStruct((2, L, C2), BF16),
        grid=(L // rb,),
        in_specs=[pl.BlockSpec((rb, H), lambda i: (i, 0)),
                  full((H, H)), full((1, H)), full((H, H)), full((1, H)), full((H, H)), full((1, H)),
                  full((1, H)),
                  full((H, C2)), full((H, C2)), full((H, C2)), full((H, C2)),
                  full((1, C2)), full((1, C2))],
        out_specs=pl.BlockSpec((2, rb, C2), lambda i: (0, i, 0)),
        compiler_params=_cparams(("parallel",)),
        name="hyena_filters",
    )(_filter_features(L), _block_diag2(fw1), dup(fb1), _block_diag2(fw2), dup(fb2), _block_diag2(fw3), dup(fb3),
      dup(ffreq), *_split_bf16(wf), *_split_bf16(wb), dl, bias.reshape(1, C2))
    return out.reshape(2 * L, C2)


def _dft_constants(L):
    N = 2 * L
    N1 = N // FFT_MINOR
    nh = N1 // 2
    k1h = N1 // 2 + 1
    k1p = -(-k1h // 8) * 8
    k1 = np.arange(k1h)[:, None]

    def stage_a(nn):
        th = 2 * np.pi * np.arange(nn)[None, :] * k1 / N1
        fa = np.zeros((2 * k1p, nn))
        fa[:k1h] = np.cos(th)
        fa[k1p:k1p + k1h] = -np.sin(th)
        return fa

    n2 = np.arange(FFT_MINOR)
    m1 = np.zeros((k1h, 2 * FFT_MINOR, 2 * FFT_MINOR))
    m2 = np.zeros_like(m1)
    for a in range(k1h):
        ph = -2 * np.pi * (n2[None, :] * a / N + n2[None, :] * n2[:, None] / FFT_MINOR)
        cr, ci = np.cos(ph), np.sin(ph)
        m1[a] = np.block([[cr, -ci], [ci, cr]])
        m2[a] = np.block([[cr.T, ci.T], [-ci.T, cr.T]])
    kk = np.arange(k1h)[None, :]
    cw = np.where((kk == 0) | (kk == N1 // 2), 1.0, 2.0) / N
    th = 2 * np.pi * np.arange(nh)[:, None] * kk / N1
    g = np.zeros((nh, 2 * k1p))
    g[:, :k1h] = cw * np.cos(th)
    g[:, k1p:k1p + k1h] = -cw * np.sin(th)
    as_bf16 = lambda a: jnp.asarray(a, F32).astype(BF16)
    return dict(nh=nh, n1=N1, k1h=k1h, k1p=k1p, fa_data=as_bf16(stage_a(nh)), fa_filt=as_bf16(stage_a(N1)),
                m1=as_bf16(m1), m2=as_bf16(m2), g=as_bf16(g))


def _lmat_kernel(f_ref, x_ref, o_ref):
    o_ref[...] = jnp.dot(f_ref[...], x_ref[...].astype(BF16), preferred_element_type=F32).astype(o_ref.dtype)


def _lmat(fmat, x2d, tn=8192):
    M, K = fmat.shape
    ncols = x2d.shape[1]
    tn = min(tn, ncols)
    return pl.pallas_call(
        _lmat_kernel,
        out_shape=jax.ShapeDtypeStruct((M, ncols), BF16),
        grid=(ncols // tn,),
        in_specs=[pl.BlockSpec((M, K), lambda j: (0, 0)),
                  pl.BlockSpec((K, tn), lambda j: (0, j))],
        out_specs=pl.BlockSpec((M, tn), lambda j: (0, j)),
        compiler_params=_cparams(("parallel",)),
        name="dft_stage_a",
    )(fmat, x2d)


def _lmat_gate_kernel(g_ref, z_ref, gate_ref, o_ref):
    o_ref[...] = jnp.dot(g_ref[...], z_ref[...], preferred_element_type=F32) * gate_ref[...]


def _lmat_gate(gmat, z2d, gate2d, tn=8192):
    M, K = gmat.shape
    ncols = z2d.shape[1]
    tn = min(tn, ncols)
    return pl.pallas_call(
        _lmat_gate_kernel,
        out_shape=jax.ShapeDtypeStruct((M, ncols), F32),
        grid=(ncols // tn,),
        in_specs=[pl.BlockSpec((M, K), lambda j: (0, 0)),
                  pl.BlockSpec((K, tn), lambda j: (0, j)),
                  pl.BlockSpec((M, tn), lambda j: (0, j))],
        out_specs=pl.BlockSpec((M, tn), lambda j: (0, j)),
        compiler_params=_cparams(("parallel",)),
        name="dft_stage_a_inv",
    )(gmat, z2d, gate2d)


def _filter_spectrum_kernel(m1_ref, a_ref, o_ref):
    x = jnp.concatenate([a_ref[0, 0], a_ref[1, 0]], axis=0)
    o_ref[0] = jnp.dot(m1_ref[0], x, preferred_element_type=F32).astype(o_ref.dtype)


def _filter_spectrum(m1, a4, k1h):
    _, k1p, n2, C = a4.shape
    return pl.pallas_call(
        _filter_spectrum_kernel,
        out_shape=jax.ShapeDtypeStruct((k1p, 2 * n2, C), BF16),
        grid=(k1p,),
        in_specs=[pl.BlockSpec((1, 2 * n2, 2 * n2), lambda a: (jnp.minimum(a, k1h - 1), 0, 0)),
                  pl.BlockSpec((2, 1, n2, C), lambda a: (0, a, 0, 0))],
        out_specs=pl.BlockSpec((1, 2 * n2, C), lambda a: (a, 0, 0)),
        compiler_params=_cparams(("parallel",)),
        name="filter_spectrum",
    )(m1, a4)


def _middle_kernel(m1_ref, m2_ref, a_ref, kf_ref, o_ref):
    n2 = a_ref.shape[2]
    x = jnp.concatenate([a_ref[0, 0], a_ref[1, 0]], axis=0)
    s = jnp.dot(m1_ref[0], x, preferred_element_type=F32)
    sr, si = s[:n2], s[n2:]
    kf = kf_ref[0].astype(F32)
    kr, ki = kf[:n2], kf[n2:]
    y = jnp.concatenate([sr * kr - si * ki, sr * ki + si * kr], axis=0).astype(BF16)
    z = jnp.dot(m2_ref[0], y, preferred_element_type=F32).astype(o_ref.dtype)
    o_ref[0, 0] = z[:n2]
    o_ref[1, 0] = z[n2:]


def _middle(m1, m2, a4, kf, order, k1h):
    _, k1p, n2, C = a4.shape
    mat = pl.BlockSpec((1, 2 * n2, 2 * n2), lambda a: (jnp.minimum(a, k1h - 1), 0, 0))
    return pl.pallas_call(
        _middle_kernel,
        out_shape=jax.ShapeDtypeStruct((2, k1p, n2, C), BF16),
        grid=(k1p,),
        in_specs=[mat, mat,
                  pl.BlockSpec((2, 1, n2, C), lambda a: (0, a, 0, 0)),
                  pl.BlockSpec((1, 2 * n2, C), lambda a: (a, 0, order))],
        out_specs=pl.BlockSpec((2, 1, n2, C), lambda a: (0, a, 0, 0)),
        compiler_params=_cparams(("parallel",)),
        name="dft_middle",
    )(m1, m2, a4, kf)


def _hyena_long_convs(v, g1, g2, kfilt, L):
    C = v.shape[1]
    cs = _dft_constants(L)
    nh, n1, k1h, k1p = cs["nh"], cs["n1"], cs["k1h"], cs["k1p"]
    fa = _lmat(cs["fa_filt"], kfilt.reshape(n1, FFT_MINOR * 2 * C))
    kspec = _filter_spectrum(cs["m1"], fa.reshape(2, k1p, FFT_MINOR, 2 * C), k1h)
    y = v
    for order, gate in enumerate((g1, g2)):
        a = _lmat(cs["fa_data"], y.reshape(nh, FFT_MINOR * C))
        z = _middle(cs["m1"], cs["m2"], a.reshape(2, k1p, FFT_MINOR, C), kspec, order, k1h)
        y2d = _lmat_gate(cs["g"], z.reshape(2 * k1p, FFT_MINOR * C), gate.reshape(nh, FFT_MINOR * C))
        y = y2d.reshape(L, C)
    return y


def _out_kernel(yh_ref, ya_ref, wh_ref, wa_ref, ng_ref, x_ref, g_ref, b_ref, o_ref):
    yh = yh_ref[...]
    yh = yh * lax.rsqrt(jnp.mean(yh * yh, axis=-1, keepdims=True) + LN_EPS) * ng_ref[...]
    acc = jnp.dot(yh.astype(BF16), wh_ref[...], preferred_element_type=F32)
    acc = acc + jnp.dot(ya_ref[...], wa_ref[...], preferred_element_type=F32)
    o_ref[...] = _layer_norm(ALPHA * x_ref[...] + acc, g_ref[...], b_ref[...])


def _out_ln(y_hy, y_at, w_out, norm_g, x1, g, b, tm=512):
    L, D = x1.shape
    ch, ca = y_hy.shape[1], y_at.shape[1]
    tm = min(tm, L)
    row = lambda c: pl.BlockSpec((tm, c), lambda i: (i, 0))
    full = lambda r, c: pl.BlockSpec((r, c), lambda i: (0, 0))
    return pl.pallas_call(
        _out_kernel,
        out_shape=jax.ShapeDtypeStruct((L, D), F32),
        grid=(L // tm,),
        in_specs=[row(ch), row(ca),
                  pl.BlockSpec((ch, D), lambda i: (0, 0)),
                  pl.BlockSpec((ca, D), lambda i: (1, 0)),
                  full(1, ch), row(D), full(1, D), full(1, D)],
        out_specs=row(D),
        compiler_params=_cparams(("parallel",)),
        name="out_ln",
    )(y_hy, y_at, w_out, w_out, norm_g, x1, g, b)


def kernel(x, ffn1_w_gate, ffn1_w_up, ffn1_w_down, ln1_g, ln1_b, w_in, hyena_conv_w, hyena_conv_b, filt_w1, filt_b1, filt_w2, filt_b2, filt_w3, filt_b3, filt_freq, filt_w_out, hyena_bias, hyena_norm_g, lambda_q1, lambda_k1, lambda_q2, lambda_k2, subln_g, w_out, ln2_g, ln2_b, ffn2_w_gate, ffn2_w_up, ffn2_w_down, ln3_g, ln3_b):
    assert x.shape[0] == 1 and ffn1_w_gate.shape[0] == DEPTH
    L = x.shape[1]
    bf = lambda a: a.astype(BF16)
    row = lambda a: a.reshape(1, -1)
    h = x[0]
    s_hy = 3 * D_HYENA
    for i in range(DEPTH):
        h = _ffn_ln(h, bf(ffn1_w_gate[i]), bf(ffn1_w_up[i]), bf(ffn1_w_down[i]), row(ln1_g[i]), row(ln1_b[i]))
        w_in_b = bf(w_in[i])
        p_hy = _proj(h, w_in_b[:, :s_hy], F32)
        q_t = _proj_q_t(h, w_in_b[:, s_hy:s_hy + D_QK].T)
        k_at = _proj_rope(h, w_in_b[:, s_hy + D_QK:s_hy + 2 * D_QK])
        v_t = _proj_v_t(h, w_in_b[:, s_hy + 2 * D_QK:].T, ATTN_TK)
        z = _short_conv(p_hy, hyena_conv_w[i], row(hyena_conv_b[i]))
        kfilt = _hyena_filters(L, filt_w1[i], filt_b1[i], filt_w2[i], filt_b2[i], filt_w3[i], filt_b3[i],
                               filt_freq[i], filt_w_out[i], hyena_bias[i])
        y_hy = _hyena_long_convs(z[:, :D_HYENA], z[:, D_HYENA:2 * D_HYENA], z[:, 2 * D_HYENA:], kfilt, L)
        y_at = _diff_attention(q_t, k_at, v_t, row(lambda_q1[i]), row(lambda_k1[i]), row(lambda_q2[i]),
                               row(lambda_k2[i]), row(subln_g[i]))
        h = _out_ln(y_hy, y_at, bf(w_out[i]), row(hyena_norm_g[i]), h, row(ln2_g[i]), row(ln2_b[i]))
        h = _ffn_ln(h, bf(ffn2_w_gate[i]), bf(ffn2_w_up[i]), bf(ffn2_w_down[i]), row(ln3_g[i]), row(ln3_b[i]))
    return h[None]
```

```python
import functools
import math

import numpy as np
import jax
import jax.numpy as jnp
from jax import lax
from jax.experimental import pallas as pl
from jax.experimental.pallas import tpu as pltpu

F32 = jnp.float32
BF16 = jnp.bfloat16

D_HYENA = 1024
N_HEADS = 8
HEAD_DIM = 64
VAL_DIM = 128
D_QK = 2 * N_HEADS * HEAD_DIM
ROT_DIM = 16
ROPE_THETA = 500000.0
FILTER_EMB = 33
FILTER_HIDDEN = 64
DECAY_TARGET = 1e-2
MIN_DECAY = math.log(DECAY_TARGET) / 1.5
MAX_DECAY = math.log(DECAY_TARGET) / 0.3
LN_EPS = 1e-5
DEPTH = 1
ALPHA = (2.0 * DEPTH) ** 0.25
LAMBDA_INIT = 0.8 - 0.6 * math.exp(-0.3 * 0)
LOG2E = math.log2(math.e)

LANES = 128
FFT_MINOR = 128
VMEM_LIMIT = 56 * 1024 * 1024
ATTN_TQ = 512
ATTN_TK = 512
FIXED_SHIFT_MIN_DENOM = 2.0 ** -60


def _cparams(sem):
    return pltpu.CompilerParams(dimension_semantics=sem, vmem_limit_bytes=VMEM_LIMIT)


def _layer_norm(y, g, b):
    mu = jnp.mean(y, axis=-1, keepdims=True)
    d = y - mu
    var = jnp.mean(d * d, axis=-1, keepdims=True)
    return d * lax.rsqrt(var + LN_EPS) * g + b


def _ffn_kernel(x_ref, wg_ref, wu_ref, wd_ref, g_ref, b_ref, o_ref, xb_ref, acc_ref, h_ref, *, nf):
    f = pl.program_id(1)

    def up(slot):
        xb = xb_ref[...]
        hg = jnp.dot(xb, wg_ref[...], preferred_element_type=F32)
        hu = jnp.dot(xb, wu_ref[...], preferred_element_type=F32)
        h_ref[slot] = (hg * jax.nn.sigmoid(hg) * hu).astype(BF16)

    def down(slot):
        acc_ref[...] += jnp.dot(h_ref[slot], wd_ref[...], preferred_element_type=F32)

    @pl.when(f == 0)
    def _():
        xb_ref[...] = x_ref[...].astype(BF16)
        acc_ref[...] = jnp.zeros_like(acc_ref)
        up(0)

    @pl.when(jnp.logical_and(f > 0, f < nf))
    def _():
        down((f - 1) % 2)
        up(f % 2)

    @pl.when(f == nf)
    def _():
        down((nf - 1) % 2)
        y = ALPHA * x_ref[...] + 0.5 * acc_ref[...]
        o_ref[...] = _layer_norm(y, g_ref[...], b_ref[...])


def _ffn_ln(x, wg, wu, wd, g, b, tm=512, tf=512):
    L, D = x.shape
    F = wg.shape[1]
    tm = min(tm, L)
    nf = F // tf
    up_w = pl.BlockSpec((D, tf), lambda i, f: (0, jnp.minimum(f, nf - 1)))
    return pl.pallas_call(
        functools.partial(_ffn_kernel, nf=nf),
        out_shape=jax.ShapeDtypeStruct((L, D), F32),
        grid=(L // tm, nf + 1),
        in_specs=[
            pl.BlockSpec((tm, D), lambda i, f: (i, 0)),
            up_w, up_w,
            pl.BlockSpec((tf, D), lambda i, f: (jnp.maximum(f - 1, 0), 0)),
            pl.BlockSpec((1, D), lambda i, f: (0, 0)),
            pl.BlockSpec((1, D), lambda i, f: (0, 0)),
        ],
        out_specs=pl.BlockSpec((tm, D), lambda i, f: (i, 0)),
        scratch_shapes=[pltpu.VMEM((tm, D), BF16), pltpu.VMEM((tm, D), F32), pltpu.VMEM((2, tm, tf), BF16)],
        compiler_params=_cparams(("parallel", "arbitrary")),
        name="ffn_ln",
    )(x, wg, wu, wd, g, b)


def _proj_kernel(x_ref, w_ref, o_ref, xb_ref):
    @pl.when(pl.program_id(1) == 0)
    def _():
        xb_ref[...] = x_ref[...].astype(BF16)

    o_ref[...] = jnp.dot(xb_ref[...], w_ref[...], preferred_element_type=F32).astype(o_ref.dtype)


def _proj(x, w, out_dtype, tm=1024, tn=1024):
    L, D = x.shape
    N = w.shape[1]
    tm = min(tm, L)
    return pl.pallas_call(
        _proj_kernel,
        out_shape=jax.ShapeDtypeStruct((L, N), out_dtype),
        grid=(L // tm, N // tn),
        in_specs=[pl.BlockSpec((tm, D), lambda i, j: (i, 0)),
                  pl.BlockSpec((D, tn), lambda i, j: (0, j))],
        out_specs=pl.BlockSpec((tm, tn), lambda i, j: (i, j)),
        scratch_shapes=[pltpu.VMEM((tm, D), BF16)],
        compiler_params=_cparams(("parallel", "arbitrary")),
        name="proj",
    )(x, w)


def _rope_tables(L, scale):
    half = ROT_DIM // 2
    inv = ROPE_THETA ** (-np.arange(0, ROT_DIM, 2, dtype=np.float64) / ROT_DIM)
    pos = np.arange(L, dtype=np.float64)[:, None]
    d = np.arange(LANES) % HEAD_DIM
    ang = pos * inv[d % half][None, :]
    c = np.where(d[None, :] < ROT_DIM, np.cos(ang), 1.0) * scale
    s1 = np.where(d[None, :] < half, -np.sin(ang), 0.0) * scale
    s2 = np.where((d[None, :] >= half) & (d[None, :] < ROT_DIM), np.sin(ang), 0.0) * scale
    return c, s1, s2


def _proj_rope_kernel(x_ref, w_ref, c_ref, s1_ref, s2_ref, o_ref, xb_ref, *, tn):
    @pl.when(pl.program_id(1) == 0)
    def _():
        xb_ref[...] = x_ref[...].astype(BF16)

    p = jnp.dot(xb_ref[...], w_ref[...], preferred_element_type=F32)
    c, s1, s2 = c_ref[...], s1_ref[...], s2_ref[...]
    shift = ROT_DIM // 2
    for h in range(tn // LANES):
        ph = p[:, h * LANES:(h + 1) * LANES]
        r = ph * c + pltpu.roll(ph, LANES - shift, 1) * s1 + pltpu.roll(ph, shift, 1) * s2
        o_ref[:, h * LANES:(h + 1) * LANES] = r.astype(o_ref.dtype)


def _proj_rope(x, w, tm=1024, tn=1024):
    L, D = x.shape
    N = w.shape[1]
    tm = min(tm, L)
    tabs = [jnp.asarray(t, F32) for t in _rope_tables(L, 1.0)]
    tab = pl.BlockSpec((tm, LANES), lambda i, j: (i, 0))
    return pl.pallas_call(
        functools.partial(_proj_rope_kernel, tn=tn),
        out_shape=jax.ShapeDtypeStruct((L, N), BF16),
        grid=(L // tm, N // tn),
        in_specs=[pl.BlockSpec((tm, D), lambda i, j: (i, 0)),
                  pl.BlockSpec((D, tn), lambda i, j: (0, j)),
                  tab, tab, tab],
        out_specs=pl.BlockSpec((tm, tn), lambda i, j: (i, j)),
        scratch_shapes=[pltpu.VMEM((tm, D), BF16)],
        compiler_params=_cparams(("parallel", "arbitrary")),
        name="proj_rope",
    )(x, w, *tabs)


def _projT_kernel(x_ref, wt_ref, *rest, rope, tn, tk):
    if rope:
        c_ref, s1_ref, s2_ref, o_ref, xb_ref = rest
    else:
        o_ref, xb_ref = rest

    @pl.when(pl.program_id(1) == 0)
    def _():
        xb_ref[...] = x_ref[...].astype(BF16)

    pt = lax.dot_general(wt_ref[...], xb_ref[...], (((1,), (1,)), ((), ())), preferred_element_type=F32)
    if rope:
        c, s1, s2 = c_ref[...], s1_ref[...], s2_ref[...]
        shift = ROT_DIM // 2
        for h in range(tn // LANES):
            ph = pt[h * LANES:(h + 1) * LANES, :]
            r = ph * c + pltpu.roll(ph, LANES - shift, 0) * s1 + pltpu.roll(ph, shift, 0) * s2
            o_ref[h * LANES:(h + 1) * LANES, :] = r.astype(o_ref.dtype)
    else:
        for s in range(pt.shape[1] // tk):
            o_ref[s] = pt[:, s * tk:(s + 1) * tk].astype(o_ref.dtype)


def _proj_q_t(x, wt, tm=1024, tn=512):
    L, D = x.shape
    N = wt.shape[0]
    tm = min(tm, L)
    tabs = [jnp.asarray(t.T, F32) for t in _rope_tables(L, HEAD_DIM ** -0.5 * LOG2E)]
    tab = pl.BlockSpec((LANES, tm), lambda i, j: (0, i))
    return pl.pallas_call(
        functools.partial(_projT_kernel, rope=True, tn=tn, tk=None),
        out_shape=jax.ShapeDtypeStruct((N, L), BF16),
        grid=(L // tm, N // tn),
        in_specs=[pl.BlockSpec((tm, D), lambda i, j: (i, 0)),
                  pl.BlockSpec((tn, D), lambda i, j: (j, 0)),
                  tab, tab, tab],
        out_specs=pl.BlockSpec((tn, tm), lambda i, j: (j, i)),
        scratch_shapes=[pltpu.VMEM((tm, D), BF16)],
        compiler_params=_cparams(("parallel", "arbitrary")),
        name="proj_q_t",
    )(x, wt, *tabs)


def _proj_v_t(x, wt, tk, tm=1024, tn=512):
    L, D = x.shape
    N = wt.shape[0]
    tm = min(tm, L)
    tk = min(tk, L)
    per = tm // tk
    return pl.pallas_call(
        functools.partial(_projT_kernel, rope=False, tn=tn, tk=tk),
        out_shape=jax.ShapeDtypeStruct((L // tk, N, tk), BF16),
        grid=(L // tm, N // tn),
        in_specs=[pl.BlockSpec((tm, D), lambda i, j: (i, 0)),
                  pl.BlockSpec((tn, D), lambda i, j: (j, 0))],
        out_specs=pl.BlockSpec((per, tn, tk), lambda i, j: (i, j, 0)),
        scratch_shapes=[pltpu.VMEM((tm, D), BF16)],
        compiler_params=_cparams(("parallel", "arbitrary")),
        name="proj_v_t",
    )(x, wt)


def _attn_kernel(qt_ref, k_ref, vt_ref, lq1_ref, lk1_ref, lq2_ref, lk2_ref, g_ref, o_ref,
                 acc_ref, l_ref, kn_ref, sa_ref, sb_ref, pa_ref, pb_ref, *, tk, nk):
    qt = qt_ref[...].astype(F32)
    tq = qt.shape[1]
    row = lax.broadcasted_iota(jnp.int32, qt.shape, 0)
    qs = (jnp.where(row < HEAD_DIM, qt, 0.0).astype(BF16),
          jnp.where(row >= HEAD_DIM, qt, 0.0).astype(BF16))
    zero = jnp.zeros((1, tq), F32)

    def k_chunk(j):
        return k_ref[pl.ds(pl.multiple_of(j * tk, tk), tk), :]

    @pl.when(pl.program_id(1) == 0)
    def _():
        def kbody(c, carry):
            kb = k_chunk(c).astype(F32)
            sq = kb * kb
            lane = lax.broadcasted_iota(jnp.int32, sq.shape, 1)
            n0 = jnp.sum(jnp.where(lane < HEAD_DIM, sq, 0.0), axis=1, keepdims=True)
            n1 = jnp.sum(jnp.where(lane >= HEAD_DIM, sq, 0.0), axis=1, keepdims=True)
            return (jnp.maximum(carry[0], jnp.max(n0, axis=0, keepdims=True)),
                    jnp.maximum(carry[1], jnp.max(n1, axis=0, keepdims=True)))

        z = jnp.zeros((1, 1), F32)
        k0, k1 = lax.fori_loop(0, nk, kbody, (z, z))
        kn_ref[0] = jnp.broadcast_to(k0, (1, LANES))
        kn_ref[1] = jnp.broadcast_to(k1, (1, LANES))

    q2 = qt * qt
    qn = (jnp.sum(q2[:HEAD_DIM], axis=0, keepdims=True), jnp.sum(q2[HEAD_DIM:], axis=0, keepdims=True))
    mb = tuple(jnp.sqrt(qn[mi] * kn_ref[mi][:, 0:1]) for mi in range(2))
    acc_ref[...] = jnp.zeros_like(acc_ref)

    def fixed_probs(j, p_ref):
        kb = k_chunk(j)
        sums = []
        for mi in range(2):
            pt = jnp.exp2(jnp.dot(kb, qs[mi], preferred_element_type=F32) - mb[mi])
            p_ref[mi] = pt.astype(BF16)
            sums.append(jnp.sum(pt, axis=0, keepdims=True))
        return tuple(sums)

    def fixed_accumulate(j, p_ref):
        vb = vt_ref[j]
        for mi in range(2):
            acc_ref[mi] += jnp.dot(vb, p_ref[mi], preferred_element_type=F32)

    def fixed_body(jj, carry):
        ls, la = carry
        j = 2 * jj
        lb = fixed_probs(j + 1, pb_ref)
        fixed_accumulate(j, pa_ref)
        ls = (ls[0] + la[0] + lb[0], ls[1] + la[1] + lb[1])
        la = fixed_probs(jnp.minimum(j + 2, nk - 1), pa_ref)
        fixed_accumulate(j + 1, pb_ref)
        return ls, la

    ls, _ = lax.fori_loop(0, nk // 2, fixed_body, ((zero, zero), fixed_probs(0, pa_ref)))
    l_ref[0] = ls[0]
    l_ref[1] = ls[1]
    trusted = jnp.min(jnp.minimum(ls[0], ls[1])) >= FIXED_SHIFT_MIN_DENOM

    @pl.when(jnp.logical_not(trusted))
    def _():
        acc_ref[...] = jnp.zeros_like(acc_ref)
        neg = jnp.full((1, tq), -jnp.inf, F32)

        def scores(j, s_ref):
            kb = k_chunk(j)
            mx = []
            for mi in range(2):
                st = jnp.dot(kb, qs[mi], preferred_element_type=F32)
                s_ref[mi] = st
                mx.append(jnp.max(st, axis=0, keepdims=True))
            return tuple(mx)

        def consume(j, s_ref, mx, ms, lr):
            vb = vt_ref[j]
            new_m, new_l = [], []
            for mi in range(2):
                m_next = jnp.maximum(ms[mi], mx[mi])
                alpha = jnp.exp2(ms[mi] - m_next)
                pt = jnp.exp2(s_ref[mi] - m_next)
                new_l.append(alpha * lr[mi] + jnp.sum(pt, axis=0, keepdims=True))
                new_m.append(m_next)
                acc_ref[mi] = acc_ref[mi] * alpha + jnp.dot(vb, pt.astype(BF16), preferred_element_type=F32)
            return tuple(new_m), tuple(new_l)

        def body(jj, carry):
            ms, lr, mxa = carry
            j = 2 * jj
            mxb = scores(j + 1, sb_ref)
            ms, lr = consume(j, sa_ref, mxa, ms, lr)
            mxa = scores(jnp.minimum(j + 2, nk - 1), sa_ref)
            ms, lr = consume(j + 1, sb_ref, mxb, ms, lr)
            return ms, lr, mxa

        _, lr, _ = lax.fori_loop(0, nk // 2, body, ((neg, neg), (zero, zero), scores(0, sa_ref)))
        l_ref[0] = lr[0]
        l_ref[1] = lr[1]

    lam = (jnp.exp(jnp.sum(lq1_ref[...] * lk1_ref[...], axis=-1, keepdims=True))
           - jnp.exp(jnp.sum(lq2_ref[...] * lk2_ref[...], axis=-1, keepdims=True)) + LAMBDA_INIT)
    ot = acc_ref[0] / l_ref[0] - lam * (acc_ref[1] / l_ref[1])
    ot = ot * lax.rsqrt(jnp.mean(ot * ot, axis=0, keepdims=True) + LN_EPS)
    o_ref[...] = (ot.T * (g_ref[...] * (1.0 - LAMBDA_INIT))).astype(o_ref.dtype)


def _diff_attention(qt, k, vt, lq1, lk1, lq2, lk2, subln_g, tq=ATTN_TQ):
    L = k.shape[0]
    nk, _, tk = vt.shape
    assert nk % 2 == 0, "the kv loop handles chunks in pairs"
    tq = min(tq, L)
    vec = pl.BlockSpec((1, HEAD_DIM), lambda h, i: (0, 0))
    return pl.pallas_call(
        functools.partial(_attn_kernel, tk=tk, nk=nk),
        out_shape=jax.ShapeDtypeStruct((L, N_HEADS * VAL_DIM), BF16),
        grid=(N_HEADS, L // tq),
        in_specs=[pl.BlockSpec((LANES, tq), lambda h, i: (h, i)),
                  pl.BlockSpec((L, LANES), lambda h, i: (0, h)),
                  pl.BlockSpec((nk, VAL_DIM, tk), lambda h, i: (0, h, 0)),
                  vec, vec, vec, vec,
                  pl.BlockSpec((1, VAL_DIM), lambda h, i: (0, 0))],
        out_specs=pl.BlockSpec((tq, VAL_DIM), lambda h, i: (i, h)),
        scratch_shapes=[pltpu.VMEM((2, VAL_DIM, tq), F32), pltpu.VMEM((2, 1, tq), F32),
                        pltpu.VMEM((2, 1, LANES), F32),
                        pltpu.VMEM((2, tk, tq), F32), pltpu.VMEM((2, tk, tq), F32),
                        pltpu.VMEM((2, tk, tq), BF16), pltpu.VMEM((2, tk, tq), BF16)],
        compiler_params=_cparams(("parallel", "arbitrary")),
        name="diff_attention",
    )(qt, k, vt, lq1, lk1, lq2, lk2, subln_g)


def _conv_kernel(x_ref, w_ref, b_ref, o_ref, *, rows, nchunks):
    w0 = w_ref[0:1, :]
    w1 = w_ref[1:2, :]
    w2 = w_ref[2:3, :]
    b = b_ref[...]
    last = nchunks * rows - 8

    def body(i, carry):
        r0 = pl.multiple_of(i * rows, rows)
        x0 = x_ref[pl.ds(r0, rows), :]
        pr = x_ref[pl.ds(pl.multiple_of(jnp.maximum(r0 - 8, 0), 8), 8), :]
        nx = x_ref[pl.ds(pl.multiple_of(jnp.minimum(r0 + rows, last), 8), 8), :]
        prev_row = jnp.where(i > 0, pr[7:8, :], 0.0)
        next_row = jnp.where(i < nchunks - 1, nx[0:1, :], 0.0)
        row = lax.broadcasted_iota(jnp.int32, x0.shape, 0)
        xm = jnp.where(row == 0, prev_row, pltpu.roll(x0, 1, 0))
        xp = jnp.where(row == rows - 1, next_row, pltpu.roll(x0, rows - 1, 0))
        o_ref[pl.ds(r0, rows), :] = xm * w0 + x0 * w1 + xp * w2 + b
        return carry

    lax.fori_loop(0, nchunks, body, 0)


def _short_conv(p, w, b, tn=256, rows=512):
    L, C = p.shape
    rows = min(rows, L)
    return pl.pallas_call(
        functools.partial(_conv_kernel, rows=rows, nchunks=L // rows),
        out_shape=jax.ShapeDtypeStruct((L, C), F32),
        grid=(C // tn,),
        in_specs=[pl.BlockSpec((L, tn), lambda j: (0, j)),
                  pl.BlockSpec((3, tn), lambda j: (0, j)),
                  pl.BlockSpec((1, tn), lambda j: (0, j))],
        out_specs=pl.BlockSpec((L, tn), lambda j: (0, j)),
        compiler_params=_cparams(("parallel",)),
        name="short_conv",
    )(p, w, b)


def _split_bf16(a):
    hi = a.astype(BF16)
    return hi, (a - hi.astype(F32)).astype(BF16)


def _filter_kernel(z_ref, w1_ref, b1_ref, w2_ref, b2_ref, w3_ref, b3_ref, fr_ref, wfh_ref, wfl_ref,
                   wbh_ref, wbl_ref, dl_ref, bias_ref, o_ref, *, L, rb):
    hp = lax.Precision.HIGHEST
    fr = fr_ref[...]
    h = jnp.sin(fr * (jnp.dot(z_ref[...], w1_ref[...], precision=hp, preferred_element_type=F32) + b1_ref[...]))
    h = jnp.sin(fr * (jnp.dot(h, w2_ref[...], precision=hp, preferred_element_type=F32) + b2_ref[...]))
    h = jnp.sin(fr * (jnp.dot(h, w3_ref[...], precision=hp, preferred_element_type=F32) + b3_ref[...]))
    hh, hl = _split_bf16(h)

    def out_proj(wh_ref, wl_ref):
        wh = wh_ref[...]
        return (jnp.dot(hh, wh, preferred_element_type=F32) + jnp.dot(hl, wh, preferred_element_type=F32)
                + jnp.dot(hh, wl_ref[...], preferred_element_type=F32))

    n = pl.program_id(0) * rb + lax.broadcasted_iota(jnp.int32, (rb, 1), 0)
    step = 1.0 / (L - 1)
    dl = dl_ref[...]
    kf = out_proj(wfh_ref, wfl_ref) * jnp.exp(-(n.astype(F32) * step) * dl)
    kf = kf + jnp.where(n == 0, bias_ref[...], 0.0)
    o_ref[0] = kf.astype(o_ref.dtype)
    kb = out_proj(wbh_ref, wbl_ref) * jnp.exp(-((L - n).astype(F32) * step) * dl)
    o_ref[1] = jnp.where(n == 0, 0.0, kb).astype(o_ref.dtype)


def _filter_features(L):
    half = LANES // 2
    bands = (FILTER_EMB - 1) // 2
    f = np.linspace(1e-4, bands - 1, bands)
    z = np.zeros((L, LANES))
    n = np.arange(L, dtype=np.float64)
    for off, pos in ((0, n), (half, L - n)):
        w = 2.0 * np.pi * pos / L
        z[:, off] = pos / (L - 1)
        z[:, off + 1:off + 1 + bands] = np.cos(f[None, :] * w[:, None])
        z[:, off + 1 + bands:off + 1 + 2 * bands] = -np.sin(f[None, :] * w[:, None])
    return jnp.asarray(z, F32)


def _pad_to(a, shape):
    return jnp.pad(a, [(0, s - d) for d, s in zip(a.shape, shape)])


def _block_diag2(a):
    half = LANES // 2
    p = _pad_to(a, (half, half))
    z = jnp.zeros_like(p)
    return jnp.concatenate([jnp.concatenate([p, z], axis=1), jnp.concatenate([z, p], axis=1)], axis=0)


def _hyena_filters(L, fw1, fb1, fw2, fb2, fw3, fb3, ffreq, fw_out, bias, rb=512):
    H = LANES
    half = H // 2
    C2 = 2 * D_HYENA
    dup = lambda v: jnp.tile(_pad_to(v[None, :], (1, half)), (1, 2))
    wo = fw_out.reshape(FILTER_HIDDEN, 2, 2, D_HYENA)
    wf = _pad_to(wo[:, :, 0, :].reshape(FILTER_HIDDEN, C2), (H, C2))
    wb = jnp.concatenate([jnp.zeros((half, C2), F32), _pad_to(wo[:, :, 1, :].reshape(FILTER_HIDDEN, C2), (half, C2))])
    deltas = np.abs(np.linspace(MIN_DECAY, MAX_DECAY, D_HYENA))
    dl = jnp.asarray(np.tile(deltas, 2)[None, :], F32)
    rb = min(rb, L)
    full = lambda shape: pl.BlockSpec(shape, lambda i: (0,) * len(shape))
    out = pl.pallas_call(
        functools.partial(_filter_kernel, L=L, rb=rb),
        out_shape=jax.ShapeDtypeStruct((2, L, C2), BF16),
        grid=(L // rb,),
        in_specs=[pl.BlockSpec((rb, H), lambda i: (i, 0)),
                  full((H, H)), full((1, H)), full((H, H)), full((1, H)), full((H, H)), full((1, H)),
                  full((1, H)),
                  full((H, C2)), full((H, C2)), full((H, C2)), full((H, C2)),
                  full((1, C2)), full((1, C2))],
        out_specs=pl.BlockSpec((2, rb, C2), lambda i: (0, i, 0)),
        compiler_params=_cparams(("parallel",)),
        name="hyena_filters",
    )(_filter_features(L), _block_diag2(fw1), dup(fb1), _block_diag2(fw2), dup(fb2), _block_diag2(fw3), dup(fb3),
      dup(ffreq), *_split_bf16(wf), *_split_bf16(wb), dl, bias.reshape(1, C2))
    return out.reshape(2 * L, C2)


def _dft_constants(L):
    N = 2 * L
    N1 = N // FFT_MINOR
    nh = N1 // 2
    k1h = N1 // 2 + 1
    k1p = -(-k1h // 8) * 8
    k1 = np.arange(k1h)[:, None]

    def stage_a(nn):
        th = 2 * np.pi * np.arange(nn)[None, :] * k1 / N1
        fa = np.zeros((2 * k1p, nn))
        fa[:k1h] = np.cos(th)
        fa[k1p:k1p + k1h] = -np.sin(th)
        return fa

    n2 = np.arange(FFT_MINOR)
    m1 = np.zeros((k1h, 2 * FFT_MINOR, 2 * FFT_MINOR))
    m2 = np.zeros_like(m1)
    for a in range(k1h):
        ph = -2 * np.pi * (n2[None, :] * a / N + n2[None, :] * n2[:, None] / FFT_MINOR)
        cr, ci = np.cos(ph), np.sin(ph)
        m1[a] = np.block([[cr, -ci], [ci, cr]])
        m2[a] = np.block([[cr.T, ci.T], [-ci.T, cr.T]])
    kk = np.arange(k1h)[None, :]
    cw = np.where((kk == 0) | (kk == N1 // 2), 1.0, 2.0) / N
    th = 2 * np.pi * np.arange(nh)[:, None] * kk / N1
    g = np.zeros((nh, 2 * k1p))
    g[:, :k1h] = cw * np.cos(th)
    g[:, k1p:k1p + k1h] = -cw * np.sin(th)
    as_bf16 = lambda a: jnp.asarray(a, F32).astype(BF16)
    return dict(nh=nh, n1=N1, k1h=k1h, k1p=k1p, fa_data=as_bf16(stage_a(nh)), fa_filt=as_bf16(stage_a(N1)),
                m1=as_bf16(m1), m2=as_bf16(m2), g=as_bf16(g))


def _lmat_kernel(f_ref, x_ref, o_ref):
    o_ref[...] = jnp.dot(f_ref[...], x_ref[...].astype(BF16), preferred_element_type=F32).astype(o_ref.dtype)


def _lmat(fmat, x2d, tn=8192):
    M, K = fmat.shape
    ncols = x2d.shape[1]
    tn = min(tn, ncols)
    return pl.pallas_call(
        _lmat_kernel,
        out_shape=jax.ShapeDtypeStruct((M, ncols), BF16),
        grid=(ncols // tn,),
        in_specs=[pl.BlockSpec((M, K), lambda j: (0, 0)),
                  pl.BlockSpec((K, tn), lambda j: (0, j))],
        out_specs=pl.BlockSpec((M, tn), lambda j: (0, j)),
        compiler_params=_cparams(("parallel",)),
        name="dft_stage_a",
    )(fmat, x2d)


def _lmat_gate_kernel(g_ref, z_ref, gate_ref, o_ref):
    o_ref[...] = jnp.dot(g_ref[...], z_ref[...], preferred_element_type=F32) * gate_ref[...]


def _lmat_gate(gmat, z2d, gate2d, tn=8192):
    M, K = gmat.shape
    ncols = z2d.shape[1]
    tn = min(tn, ncols)
    return pl.pallas_call(
        _lmat_gate_kernel,
        out_shape=jax.ShapeDtypeStruct((M, ncols), F32),
        grid=(ncols // tn,),
        in_specs=[pl.BlockSpec((M, K), lambda j: (0, 0)),
                  pl.BlockSpec((K, tn), lambda j: (0, j)),
                  pl.BlockSpec((M, tn), lambda j: (0, j))],
        out_specs=pl.BlockSpec((M, tn), lambda j: (0, j)),
        compiler_params=_cparams(("parallel",)),
        name="dft_stage_a_inv",
    )(gmat, z2d, gate2d)


def _filter_spectrum_kernel(m1_ref, a_ref, o_ref):
    x = jnp.concatenate([a_ref[0, 0], a_ref[1, 0]], axis=0)
    o_ref[0] = jnp.dot(m1_ref[0], x, preferred_element_type=F32).astype(o_ref.dtype)


def _filter_spectrum(m1, a4, k1h):
    _, k1p, n2, C = a4.shape
    return pl.pallas_call(
        _filter_spectrum_kernel,
        out_shape=jax.ShapeDtypeStruct((k1p, 2 * n2, C), BF16),
        grid=(k1p,),
        in_specs=[pl.BlockSpec((1, 2 * n2, 2 * n2), lambda a: (jnp.minimum(a, k1h - 1), 0, 0)),
                  pl.BlockSpec((2, 1, n2, C), lambda a: (0, a, 0, 0))],
        out_specs=pl.BlockSpec((1, 2 * n2, C), lambda a: (a, 0, 0)),
        compiler_params=_cparams(("parallel",)),
        name="filter_spectrum",
    )(m1, a4)


def _middle_kernel(m1_ref, m2_ref, a_ref, kf_ref, o_ref):
    n2 = a_ref.shape[2]
    x = jnp.concatenate([a_ref[0, 0], a_ref[1, 0]], axis=0)
    s = jnp.dot(m1_ref[0], x, preferred_element_type=F32)
    sr, si = s[:n2], s[n2:]
    kf = kf_ref[0].astype(F32)
    kr, ki = kf[:n2], kf[n2:]
    y = jnp.concatenate([sr * kr - si * ki, sr * ki + si * kr], axis=0).astype(BF16)
    z = jnp.dot(m2_ref[0], y, preferred_element_type=F32).astype(o_ref.dtype)
    o_ref[0, 0] = z[:n2]
    o_ref[1, 0] = z[n2:]


def _middle(m1, m2, a4, kf, order, k1h):
    _, k1p, n2, C = a4.shape
    mat = pl.BlockSpec((1, 2 * n2, 2 * n2), lambda a: (jnp.minimum(a, k1h - 1), 0, 0))
    return pl.pallas_call(
        _middle_kernel,
        out_shape=jax.ShapeDtypeStruct((2, k1p, n2, C), BF16),
        grid=(k1p,),
        in_specs=[mat, mat,
                  pl.BlockSpec((2, 1, n2, C), lambda a: (0, a, 0, 0)),
                  pl.BlockSpec((1, 2 * n2, C), lambda a: (a, 0, order))],
        out_specs=pl.BlockSpec((2, 1, n2, C), lambda a: (0, a, 0, 0)),
        compiler_params=_cparams(("parallel",)),
        name="dft_middle",
    )(m1, m2, a4, kf)


def _hyena_long_convs(v, g1, g2, kfilt, L):
    C = v.shape[1]
    cs = _dft_constants(L)
    nh, n1, k1h, k1p = cs["nh"], cs["n1"], cs["k1h"], cs["k1p"]
    fa = _lmat(cs["fa_filt"], kfilt.reshape(n1, FFT_MINOR * 2 * C))
    kspec = _filter_spectrum(cs["m1"], fa.reshape(2, k1p, FFT_MINOR, 2 * C), k1h)
    y = v
    for order, gate in enumerate((g1, g2)):
        a = _lmat(cs["fa_data"], y.reshape(nh, FFT_MINOR * C))
        z = _middle(cs["m1"], cs["m2"], a.reshape(2, k1p, FFT_MINOR, C), kspec, order, k1h)
        y2d = _lmat_gate(cs["g"], z.reshape(2 * k1p, FFT_MINOR * C), gate.reshape(nh, FFT_MINOR * C))
        y = y2d.reshape(L, C)
    return y


def _out_kernel(yh_ref, ya_ref, wh_ref, wa_ref, ng_ref, x_ref, g_ref, b_ref, o_ref):
    yh = yh_ref[...]
    yh = yh * lax.rsqrt(jnp.mean(yh * yh, axis=-1, keepdims=True) + LN_EPS) * ng_ref[...]
    acc = jnp.dot(yh.astype(BF16), wh_ref[...], preferred_element_type=F32)
    acc = acc + jnp.dot(ya_ref[...], wa_ref[...], preferred_element_type=F32)
    o_ref[...] = _layer_norm(ALPHA * x_ref[...] + acc, g_ref[...], b_ref[...])


def _out_ln(y_hy, y_at, w_out, norm_g, x1, g, b, tm=512):
    L, D = x1.shape
    ch, ca = y_hy.shape[1], y_at.shape[1]
    tm = min(tm, L)
    row = lambda c: pl.BlockSpec((tm, c), lambda i: (i, 0))
    full = lambda r, c: pl.BlockSpec((r, c), lambda i: (0, 0))
    return pl.pallas_call(
        _out_kernel,
        out_shape=jax.ShapeDtypeStruct((L, D), F32),
        grid=(L // tm,),
        in_specs=[row(ch), row(ca),
                  pl.BlockSpec((ch, D), lambda i: (0, 0)),
                  pl.BlockSpec((ca, D), lambda i: (1, 0)),
                  full(1, ch), row(D), full(1, D), full(1, D)],
        out_specs=row(D),
        compiler_params=_cparams(("parallel",)),
        name="out_ln",
    )(y_hy, y_at, w_out, w_out, norm_g, x1, g, b)


def kernel(x, ffn1_w_gate, ffn1_w_up, ffn1_w_down, ln1_g, ln1_b, w_in, hyena_conv_w, hyena_conv_b, filt_w1, filt_b1, filt_w2, filt_b2, filt_w3, filt_b3, filt_freq, filt_w_out, hyena_bias, hyena_norm_g, lambda_q1, lambda_k1, lambda_q2, lambda_k2, subln_g, w_out, ln2_g, ln2_b, ffn2_w_gate, ffn2_w_up, ffn2_w_down, ln3_g, ln3_b):
    assert x.shape[0] == 1 and ffn1_w_gate.shape[0] == DEPTH
    L = x.shape[1]
    bf = lambda a: a.astype(BF16)
    row = lambda a: a.reshape(1, -1)
    h = x[0]
    s_hy = 3 * D_HYENA
    for i in range(DEPTH):
        h = _ffn_ln(h, bf(ffn1_w_gate[i]), bf(ffn1_w_up[i]), bf(ffn1_w_down[i]), row(ln1_g[i]), row(ln1_b[i]))
        w_in_b = bf(w_in[i])
        p_hy = _proj(h, w_in_b[:, :s_hy], F32)
        q_t = _proj_q_t(h, w_in_b[:, s_hy:s_hy + D_QK].T)
        k_at = _proj_rope(h, w_in_b[:, s_hy + D_QK:s_hy + 2 * D_QK])
        v_t = _proj_v_t(h, w_in_b[:, s_hy + 2 * D_QK:].T, ATTN_TK)
        z = _short_conv(p_hy, hyena_conv_w[i], row(hyena_conv_b[i]))
        kfilt = _hyena_filters(L, filt_w1[i], filt_b1[i], filt_w2[i], filt_b2[i], filt_w3[i], filt_b3[i],
                               filt_freq[i], filt_w_out[i], hyena_bias[i])
        y_hy = _hyena_long_convs(z[:, :D_HYENA], z[:, D_HYENA:2 * D_HYENA], z[:, 2 * D_HYENA:], kfilt, L)
        y_at = _diff_attention(q_t, k_at, v_t, row(lambda_q1[i]), row(lambda_k1[i]), row(lambda_q2[i]),
                               row(lambda_k2[i]), row(subln_g[i]))
        h = _out_ln(y_hy, y_at, bf(w_out[i]), row(hyena_norm_g[i]), h, row(ln2_g[i]), row(ln2_b[i]))
        h = _ffn_ln(h, bf(ffn2_w_gate[i]), bf(ffn2_w_up[i]), bf(ffn2_w_down[i]), row(ln3_g[i]), row(ln3_b[i]))
    return h[None]
```

```python
import functools
import math

import numpy as np
import jax
import jax.numpy as jnp
from jax import lax
from jax.experimental import pallas as pl
from jax.experimental.pallas import tpu as pltpu

F32 = jnp.float32
BF16 = jnp.bfloat16

D_HYENA = 1024
N_HEADS = 8
HEAD_DIM = 64
VAL_DIM = 128
D_QK = 2 * N_HEADS * HEAD_DIM
ROT_DIM = 16
ROPE_THETA = 500000.0
FILTER_EMB = 33
FILTER_HIDDEN = 64
DECAY_TARGET = 1e-2
MIN_DECAY = math.log(DECAY_TARGET) / 1.5
MAX_DECAY = math.log(DECAY_TARGET) / 0.3
LN_EPS = 1e-5
DEPTH = 1
ALPHA = (2.0 * DEPTH) ** 0.25
LAMBDA_INIT = 0.8 - 0.6 * math.exp(-0.3 * 0)
LOG2E = math.log2(math.e)

LANES = 128
FFT_MINOR = 128
VMEM_LIMIT = 56 * 1024 * 1024
ATTN_TQ = 512
ATTN_TK = 512
FIXED_SHIFT_MIN_DENOM = 2.0 ** -60


def _cparams(sem):
    return pltpu.CompilerParams(dimension_semantics=sem, vmem_limit_bytes=VMEM_LIMIT)


def _layer_norm(y, g, b):
    mu = jnp.mean(y, axis=-1, keepdims=True)
    d = y - mu
    var = jnp.mean(d * d, axis=-1, keepdims=True)
    return d * lax.rsqrt(var + LN_EPS) * g + b


def _ffn_kernel(x_ref, wg_ref, wu_ref, wd_ref, g_ref, b_ref, o_ref, xb_ref, acc_ref, h_ref, *, nf):
    f = pl.program_id(1)

    def up(slot):
        xb = xb_ref[...]
        hg = jnp.dot(xb, wg_ref[0], preferred_element_type=F32)
        hu = jnp.dot(xb, wu_ref[0], preferred_element_type=F32)
        h_ref[slot] = (hg * jax.nn.sigmoid(hg) * hu).astype(BF16)

    def down(slot):
        acc_ref[...] += jnp.dot(h_ref[slot], wd_ref[...], preferred_element_type=F32)

    @pl.when(f == 0)
    def _():
        xb_ref[...] = x_ref[...].astype(BF16)
        acc_ref[...] = jnp.zeros_like(acc_ref)
        up(0)

    @pl.when(jnp.logical_and(f > 0, f < nf))
    def _():
        down((f - 1) % 2)
        up(f % 2)

    @pl.when(f == nf)
    def _():
        down((nf - 1) % 2)
        y = ALPHA * x_ref[...] + 0.5 * acc_ref[...]
        o_ref[...] = _layer_norm(y, g_ref[...], b_ref[...])


FFN_TF = 512


def _ffn_up_blocks(w):
    D, F = w.shape
    return w.astype(BF16).reshape(D, F // FFN_TF, FFN_TF).transpose(1, 0, 2)


def _ffn_ln(x, wg, wu, wd, g, b, tm=1024):
    L, D = x.shape
    nf, _, tf = wg.shape
    tm = min(tm, L)
    up_w = pl.BlockSpec((1, D, tf), lambda i, f: (jnp.minimum(f, nf - 1), 0, 0))
    once = dict(pipeline_mode=pl.Buffered(1))
    return pl.pallas_call(
        functools.partial(_ffn_kernel, nf=nf),
        out_shape=jax.ShapeDtypeStruct((L, D), F32),
        grid=(L // tm, nf + 1),
        in_specs=[
            pl.BlockSpec((tm, D), lambda i, f: (i, 0), **once),
            up_w, up_w,
            pl.BlockSpec((tf, D), lambda i, f: (jnp.maximum(f - 1, 0), 0)),
            pl.BlockSpec((1, D), lambda i, f: (0, 0)),
            pl.BlockSpec((1, D), lambda i, f: (0, 0)),
        ],
        out_specs=pl.BlockSpec((tm, D), lambda i, f: (i, 0), **once),
        scratch_shapes=[pltpu.VMEM((tm, D), BF16), pltpu.VMEM((tm, D), F32), pltpu.VMEM((2, tm, tf), BF16)],
        compiler_params=_cparams(("parallel", "arbitrary")),
        name="ffn_ln",
    )(x, wg, wu, wd, g, b)


def _proj_kernel(x_ref, w_ref, o_ref, xb_ref):
    @pl.when(pl.program_id(1) == 0)
    def _():
        xb_ref[...] = x_ref[...].astype(BF16)

    o_ref[...] = jnp.dot(xb_ref[...], w_ref[...], preferred_element_type=F32).astype(o_ref.dtype)


def _proj(x, w, out_dtype, tm=1024, tn=1024):
    L, D = x.shape
    N = w.shape[1]
    tm = min(tm, L)
    return pl.pallas_call(
        _proj_kernel,
        out_shape=jax.ShapeDtypeStruct((L, N), out_dtype),
        grid=(L // tm, N // tn),
        in_specs=[pl.BlockSpec((tm, D), lambda i, j: (i, 0)),
                  pl.BlockSpec((D, tn), lambda i, j: (0, j))],
        out_specs=pl.BlockSpec((tm, tn), lambda i, j: (i, j)),
        scratch_shapes=[pltpu.VMEM((tm, D), BF16)],
        compiler_params=_cparams(("parallel", "arbitrary")),
        name="proj",
    )(x, w)


def _rope_tables(L, scale):
    half = ROT_DIM // 2
    inv = ROPE_THETA ** (-np.arange(0, ROT_DIM, 2, dtype=np.float64) / ROT_DIM)
    pos = np.arange(L, dtype=np.float64)[:, None]
    d = np.arange(LANES) % HEAD_DIM
    ang = pos * inv[d % half][None, :]
    c = np.where(d[None, :] < ROT_DIM, np.cos(ang), 1.0) * scale
    s1 = np.where(d[None, :] < half, -np.sin(ang), 0.0) * scale
    s2 = np.where((d[None, :] >= half) & (d[None, :] < ROT_DIM), np.sin(ang), 0.0) * scale
    return c, s1, s2


def _proj_rope_kernel(x_ref, w_ref, c_ref, s1_ref, s2_ref, o_ref, xb_ref, *, tn):
    @pl.when(pl.program_id(1) == 0)
    def _():
        xb_ref[...] = x_ref[...].astype(BF16)

    p = jnp.dot(xb_ref[...], w_ref[...], preferred_element_type=F32)
    c, s1, s2 = c_ref[...], s1_ref[...], s2_ref[...]
    shift = ROT_DIM // 2
    for h in range(tn // LANES):
        ph = p[:, h * LANES:(h + 1) * LANES]
        r = ph * c + pltpu.roll(ph, LANES - shift, 1) * s1 + pltpu.roll(ph, shift, 1) * s2
        o_ref[:, h * LANES:(h + 1) * LANES] = r.astype(o_ref.dtype)


def _proj_rope(x, w, tm=1024, tn=1024):
    L, D = x.shape
    N = w.shape[1]
    tm = min(tm, L)
    tabs = [jnp.asarray(t, F32) for t in _rope_tables(L, 1.0)]
    tab = pl.BlockSpec((tm, LANES), lambda i, j: (i, 0))
    return pl.pallas_call(
        functools.partial(_proj_rope_kernel, tn=tn),
        out_shape=jax.ShapeDtypeStruct((L, N), BF16),
        grid=(L // tm, N // tn),
        in_specs=[pl.BlockSpec((tm, D), lambda i, j: (i, 0)),
                  pl.BlockSpec((D, tn), lambda i, j: (0, j)),
                  tab, tab, tab],
        out_specs=pl.BlockSpec((tm, tn), lambda i, j: (i, j)),
        scratch_shapes=[pltpu.VMEM((tm, D), BF16)],
        compiler_params=_cparams(("parallel", "arbitrary")),
        name="proj_rope",
    )(x, w, *tabs)


def _projT_kernel(x_ref, wt_ref, *rest, rope, tn, tk):
    if rope:
        c_ref, s1_ref, s2_ref, o_ref, xb_ref = rest
    else:
        o_ref, xb_ref = rest

    @pl.when(pl.program_id(1) == 0)
    def _():
        xb_ref[...] = x_ref[...].astype(BF16)

    pt = lax.dot_general(wt_ref[...], xb_ref[...], (((1,), (1,)), ((), ())), preferred_element_type=F32)
    if rope:
        c, s1, s2 = c_ref[...], s1_ref[...], s2_ref[...]
        shift = ROT_DIM // 2
        for h in range(tn // LANES):
            ph = pt[h * LANES:(h + 1) * LANES, :]
            r = ph * c + pltpu.roll(ph, LANES - shift, 0) * s1 + pltpu.roll(ph, shift, 0) * s2
            o_ref[h * LANES:(h + 1) * LANES, :] = r.astype(o_ref.dtype)
    else:
        for s in range(pt.shape[1] // tk):
            o_ref[s] = pt[:, s * tk:(s + 1) * tk].astype(o_ref.dtype)


def _proj_q_t(x, wt, tm=1024, tn=512):
    L, D = x.shape
    N = wt.shape[0]
    tm = min(tm, L)
    tabs = [jnp.asarray(t.T, F32) for t in _rope_tables(L, HEAD_DIM ** -0.5 * LOG2E)]
    tab = pl.BlockSpec((LANES, tm), lambda i, j: (0, i))
    return pl.pallas_call(
        functools.partial(_projT_kernel, rope=True, tn=tn, tk=None),
        out_shape=jax.ShapeDtypeStruct((N, L), BF16),
        grid=(L // tm, N // tn),
        in_specs=[pl.BlockSpec((tm, D), lambda i, j: (i, 0)),
                  pl.BlockSpec((tn, D), lambda i, j: (j, 0)),
                  tab, tab, tab],
        out_specs=pl.BlockSpec((tn, tm), lambda i, j: (j, i)),
        scratch_shapes=[pltpu.VMEM((tm, D), BF16)],
        compiler_params=_cparams(("parallel", "arbitrary")),
        name="proj_q_t",
    )(x, wt, *tabs)


def _proj_v_t(x, wt, tk, tm=1024, tn=512):
    L, D = x.shape
    N = wt.shape[0]
    tm = min(tm, L)
    tk = min(tk, L)
    per = tm // tk
    return pl.pallas_call(
        functools.partial(_projT_kernel, rope=False, tn=tn, tk=tk),
        out_shape=jax.ShapeDtypeStruct((L // tk, N, tk), BF16),
        grid=(L // tm, N // tn),
        in_specs=[pl.BlockSpec((tm, D), lambda i, j: (i, 0)),
                  pl.BlockSpec((tn, D), lambda i, j: (j, 0))],
        out_specs=pl.BlockSpec((per, tn, tk), lambda i, j: (i, j, 0)),
        scratch_shapes=[pltpu.VMEM((tm, D), BF16)],
        compiler_params=_cparams(("parallel", "arbitrary")),
        name="proj_v_t",
    )(x, wt)


def _attn_kernel(qt_ref, k_ref, vt_ref, lq1_ref, lk1_ref, lq2_ref, lk2_ref, g_ref, o_ref,
                 acc_ref, l_ref, kn_ref, sa_ref, sb_ref, pa_ref, pb_ref, *, tk, nk):
    qt = qt_ref[...].astype(F32)
    tq = qt.shape[1]
    row = lax.broadcasted_iota(jnp.int32, qt.shape, 0)
    qs = (jnp.where(row < HEAD_DIM, qt, 0.0).astype(BF16),
          jnp.where(row >= HEAD_DIM, qt, 0.0).astype(BF16))
    zero = jnp.zeros((1, tq), F32)

    def k_chunk(j):
        return k_ref[pl.ds(pl.multiple_of(j * tk, tk), tk), :]

    @pl.when(pl.program_id(1) == 0)
    def _():
        def kbody(c, carry):
            kb = k_chunk(c).astype(F32)
            sq = kb * kb
            lane = lax.broadcasted_iota(jnp.int32, sq.shape, 1)
            n0 = jnp.sum(jnp.where(lane < HEAD_DIM, sq, 0.0), axis=1, keepdims=True)
            n1 = jnp.sum(jnp.where(lane >= HEAD_DIM, sq, 0.0), axis=1, keepdims=True)
            return (jnp.maximum(carry[0], jnp.max(n0, axis=0, keepdims=True)),
                    jnp.maximum(carry[1], jnp.max(n1, axis=0, keepdims=True)))

        z = jnp.zeros((1, 1), F32)
        k0, k1 = lax.fori_loop(0, nk, kbody, (z, z))
        kn_ref[0] = jnp.broadcast_to(k0, (1, LANES))
        kn_ref[1] = jnp.broadcast_to(k1, (1, LANES))

    q2 = qt * qt
    qn = (jnp.sum(q2[:HEAD_DIM], axis=0, keepdims=True), jnp.sum(q2[HEAD_DIM:], axis=0, keepdims=True))
    mb = tuple(jnp.sqrt(qn[mi] * kn_ref[mi][:, 0:1]) for mi in range(2))
    acc_ref[...] = jnp.zeros_like(acc_ref)

    def fixed_probs(j, p_ref):
        kb = k_chunk(j)
        sums = []
        for mi in range(2):
            pt = jnp.exp2(jnp.dot(kb, qs[mi], preferred_element_type=F32) - mb[mi])
            p_ref[mi] = pt.astype(BF16)
            sums.append(jnp.sum(pt, axis=0, keepdims=True))
        return tuple(sums)

    def fixed_accumulate(j, p_ref):
        vb = vt_ref[j]
        for mi in range(2):
            acc_ref[mi] += jnp.dot(vb, p_ref[mi], preferred_element_type=F32)

    def fixed_body(jj, carry):
        ls, la = carry
        j = 2 * jj
        lb = fixed_probs(j + 1, pb_ref)
        fixed_accumulate(j, pa_ref)
        ls = (ls[0] + la[0] + lb[0], ls[1] + la[1] + lb[1])
        la = fixed_probs(jnp.minimum(j + 2, nk - 1), pa_ref)
        fixed_accumulate(j + 1, pb_ref)
        return ls, la

    ls, _ = lax.fori_loop(0, nk // 2, fixed_body, ((zero, zero), fixed_probs(0, pa_ref)))
    l_ref[0] = ls[0]
    l_ref[1] = ls[1]
    trusted = jnp.min(jnp.minimum(ls[0], ls[1])) >= FIXED_SHIFT_MIN_DENOM

    @pl.when(jnp.logical_not(trusted))
    def _():
        acc_ref[...] = jnp.zeros_like(acc_ref)
        neg = jnp.full((1, tq), -jnp.inf, F32)

        def scores(j, s_ref):
            kb = k_chunk(j)
            mx = []
            for mi in range(2):
                st = jnp.dot(kb, qs[mi], preferred_element_type=F32)
                s_ref[mi] = st
                mx.append(jnp.max(st, axis=0, keepdims=True))
            return tuple(mx)

        def consume(j, s_ref, mx, ms, lr):
            vb = vt_ref[j]
            new_m, new_l = [], []
            for mi in range(2):
                m_next = jnp.maximum(ms[mi], mx[mi])
                alpha = jnp.exp2(ms[mi] - m_next)
                pt = jnp.exp2(s_ref[mi] - m_next)
                new_l.append(alpha * lr[mi] + jnp.sum(pt, axis=0, keepdims=True))
                new_m.append(m_next)
                acc_ref[mi] = acc_ref[mi] * alpha + jnp.dot(vb, pt.astype(BF16), preferred_element_type=F32)
            return tuple(new_m), tuple(new_l)

        def body(jj, carry):
            ms, lr, mxa = carry
            j = 2 * jj
            mxb = scores(j + 1, sb_ref)
            ms, lr = consume(j, sa_ref, mxa, ms, lr)
            mxa = scores(jnp.minimum(j + 2, nk - 1), sa_ref)
            ms, lr = consume(j + 1, sb_ref, mxb, ms, lr)
            return ms, lr, mxa

        _, lr, _ = lax.fori_loop(0, nk // 2, body, ((neg, neg), (zero, zero), scores(0, sa_ref)))
        l_ref[0] = lr[0]
        l_ref[1] = lr[1]

    lam = (jnp.exp(jnp.sum(lq1_ref[...] * lk1_ref[...], axis=-1, keepdims=True))
           - jnp.exp(jnp.sum(lq2_ref[...] * lk2_ref[...], axis=-1, keepdims=True)) + LAMBDA_INIT)
    ot = acc_ref[0] / l_ref[0] - lam * (acc_ref[1] / l_ref[1])
    ot = ot * lax.rsqrt(jnp.mean(ot * ot, axis=0, keepdims=True) + LN_EPS)
    o_ref[...] = (ot.T * (g_ref[...] * (1.0 - LAMBDA_INIT))).astype(o_ref.dtype)


def _diff_attention(qt, k, vt, lq1, lk1, lq2, lk2, subln_g, tq=ATTN_TQ):
    L = k.shape[0]
    nk, _, tk = vt.shape
    assert nk % 2 == 0, "the kv loop handles chunks in pairs"
    tq = min(tq, L)
    vec = pl.BlockSpec((1, HEAD_DIM), lambda h, i: (0, 0))
    return pl.pallas_call(
        functools.partial(_attn_kernel, tk=tk, nk=nk),
        out_shape=jax.ShapeDtypeStruct((L, N_HEADS * VAL_DIM), BF16),
        grid=(N_HEADS, L // tq),
        in_specs=[pl.BlockSpec((LANES, tq), lambda h, i: (h, i)),
                  pl.BlockSpec((L, LANES), lambda h, i: (0, h)),
                  pl.BlockSpec((nk, VAL_DIM, tk), lambda h, i: (0, h, 0)),
                  vec, vec, vec, vec,
                  pl.BlockSpec((1, VAL_DIM), lambda h, i: (0, 0))],
        out_specs=pl.BlockSpec((tq, VAL_DIM), lambda h, i: (i, h)),
        scratch_shapes=[pltpu.VMEM((2, VAL_DIM, tq), F32), pltpu.VMEM((2, 1, tq), F32),
                        pltpu.VMEM((2, 1, LANES), F32),
                        pltpu.VMEM((2, tk, tq), F32), pltpu.VMEM((2, tk, tq), F32),
                        pltpu.VMEM((2, tk, tq), BF16), pltpu.VMEM((2, tk, tq), BF16)],
        compiler_params=_cparams(("parallel", "arbitrary")),
        name="diff_attention",
    )(qt, k, vt, lq1, lk1, lq2, lk2, subln_g)


def _conv_kernel(x_ref, w_ref, b_ref, o_ref, *, rows, nchunks):
    w0 = w_ref[0:1, :]
    w1 = w_ref[1:2, :]
    w2 = w_ref[2:3, :]
    b = b_ref[...]
    last = nchunks * rows - 8

    def body(i, carry):
        r0 = pl.multiple_of(i * rows, rows)
        x0 = x_ref[pl.ds(r0, rows), :]
        pr = x_ref[pl.ds(pl.multiple_of(jnp.maximum(r0 - 8, 0), 8), 8), :]
        nx = x_ref[pl.ds(pl.multiple_of(jnp.minimum(r0 + rows, last), 8), 8), :]
        prev_row = jnp.where(i > 0, pr[7:8, :], 0.0)
        next_row = jnp.where(i < nchunks - 1, nx[0:1, :], 0.0)
        row = lax.broadcasted_iota(jnp.int32, x0.shape, 0)
        xm = jnp.where(row == 0, prev_row, pltpu.roll(x0, 1, 0))
        xp = jnp.where(row == rows - 1, next_row, pltpu.roll(x0, rows - 1, 0))
        o_ref[pl.ds(r0, rows), :] = xm * w0 + x0 * w1 + xp * w2 + b
        return carry

    lax.fori_loop(0, nchunks, body, 0)


def _short_conv(p, w, b, tn=256, rows=512):
    L, C = p.shape
    rows = min(rows, L)
    return pl.pallas_call(
        functools.partial(_conv_kernel, rows=rows, nchunks=L // rows),
        out_shape=jax.ShapeDtypeStruct((L, C), F32),
        grid=(C // tn,),
        in_specs=[pl.BlockSpec((L, tn), lambda j: (0, j)),
                  pl.BlockSpec((3, tn), lambda j: (0, j)),
                  pl.BlockSpec((1, tn), lambda j: (0, j))],
        out_specs=pl.BlockSpec((L, tn), lambda j: (0, j)),
        compiler_params=_cparams(("parallel",)),
        name="short_conv",
    )(p, w, b)


def _split_bf16(a):
    hi = a.astype(BF16)
    return hi, (a - hi.astype(F32)).astype(BF16)


def _filter_kernel(z_ref, w1_ref, b1_ref, w2_ref, b2_ref, w3_ref, b3_ref, fr_ref, wfh_ref, wfl_ref,
                   wbh_ref, wbl_ref, dl_ref, bias_ref, o_ref, *, L, nh):
    hp = lax.Precision.HIGHEST
    fr = fr_ref[...]
    h = jnp.sin(fr * (jnp.dot(z_ref[...], w1_ref[...], precision=hp, preferred_element_type=F32) + b1_ref[...]))
    h = jnp.sin(fr * (jnp.dot(h, w2_ref[...], precision=hp, preferred_element_type=F32) + b2_ref[...]))
    h = jnp.sin(fr * (jnp.dot(h, w3_ref[...], precision=hp, preferred_element_type=F32) + b3_ref[...]))
    hh, hl = _split_bf16(h)

    def out_proj(wh_ref, wl_ref):
        wh = wh_ref[...]
        return (jnp.dot(hh, wh, preferred_element_type=F32) + jnp.dot(hl, wh, preferred_element_type=F32)
                + jnp.dot(hh, wl_ref[...], preferred_element_type=F32))

    rb = z_ref.shape[0]
    nq = rb // nh
    c2 = dl_ref.shape[1]
    row = lax.broadcasted_iota(jnp.int32, (rb, 1), 0)
    assert nh & (nh - 1) == 0
    r = jnp.bitwise_and(row, nh - 1)
    q_idx = lax.shift_right_logical(row, nh.bit_length() - 1)
    n = FFT_MINOR * r + nq * pl.program_id(0) + q_idx
    step = 1.0 / (L - 1)
    dl = dl_ref[...]
    kf = out_proj(wfh_ref, wfl_ref) * jnp.exp(-(n.astype(F32) * step) * dl)
    kf = (kf + jnp.where(n == 0, bias_ref[...], 0.0)).astype(o_ref.dtype)
    kb = out_proj(wbh_ref, wbl_ref) * jnp.exp(-((L - n).astype(F32) * step) * dl)
    kb = jnp.where(n == 0, 0.0, kb).astype(o_ref.dtype)
    for q in range(nq):
        o_ref[0:nh, q * c2:(q + 1) * c2] = kf[q * nh:(q + 1) * nh]
        o_ref[nh:2 * nh, q * c2:(q + 1) * c2] = kb[q * nh:(q + 1) * nh]


def _filter_features(L, nq):
    half = LANES // 2
    nh = L // FFT_MINOR
    bands = (FILTER_EMB - 1) // 2
    f = np.linspace(1e-4, bands - 1, bands)
    z = np.zeros((L, LANES))
    step, q, r = np.meshgrid(np.arange(FFT_MINOR // nq), np.arange(nq), np.arange(nh), indexing="ij")
    n = (FFT_MINOR * r + nq * step + q).reshape(-1).astype(np.float64)
    for off, pos in ((0, n), (half, L - n)):
        w = 2.0 * np.pi * pos / L
        z[:, off] = pos / (L - 1)
        z[:, off + 1:off + 1 + bands] = np.cos(f[None, :] * w[:, None])
        z[:, off + 1 + bands:off + 1 + 2 * bands] = -np.sin(f[None, :] * w[:, None])
    return jnp.asarray(z, F32)


def _pad_to(a, shape):
    return jnp.pad(a, [(0, s - d) for d, s in zip(a.shape, shape)])


def _block_diag2(a):
    half = LANES // 2
    p = _pad_to(a, (half, half))
    z = jnp.zeros_like(p)
    return jnp.concatenate([jnp.concatenate([p, z], axis=1), jnp.concatenate([z, p], axis=1)], axis=0)


def _hyena_filters(L, fw1, fb1, fw2, fb2, fw3, fb3, ffreq, fw_out, bias, nq=8):
    H = LANES
    half = H // 2
    C2 = 2 * D_HYENA
    nh = L // FFT_MINOR
    rb = nq * nh
    dup = lambda v: jnp.tile(_pad_to(v[None, :], (1, half)), (1, 2))
    wo = fw_out.reshape(FILTER_HIDDEN, 2, 2, D_HYENA)
    wf = _pad_to(wo[:, :, 0, :].reshape(FILTER_HIDDEN, C2), (H, C2))
    wb = jnp.concatenate([jnp.zeros((half, C2), F32), _pad_to(wo[:, :, 1, :].reshape(FILTER_HIDDEN, C2), (half, C2))])
    deltas = np.abs(np.linspace(MIN_DECAY, MAX_DECAY, D_HYENA))
    dl = jnp.asarray(np.tile(deltas, 2)[None, :], F32)
    full = lambda shape: pl.BlockSpec(shape, lambda i: (0,) * len(shape))
    return pl.pallas_call(
        functools.partial(_filter_kernel, L=L, nh=nh),
        out_shape=jax.ShapeDtypeStruct((2 * nh, FFT_MINOR * C2), BF16),
        grid=(L // rb,),
        in_specs=[pl.BlockSpec((rb, H), lambda i: (i, 0)),
                  full((H, H)), full((1, H)), full((H, H)), full((1, H)), full((H, H)), full((1, H)),
                  full((1, H)),
                  full((H, C2)), full((H, C2)), full((H, C2)), full((H, C2)),
                  full((1, C2)), full((1, C2))],
        out_specs=pl.BlockSpec((2 * nh, nq * C2), lambda i: (0, i)),
        compiler_params=_cparams(("parallel",)),
        name="hyena_filters",
    )(_filter_features(L, nq), _block_diag2(fw1), dup(fb1), _block_diag2(fw2), dup(fb2), _block_diag2(fw3),
      dup(fb3), dup(ffreq), *_split_bf16(wf), *_split_bf16(wb), dl, bias.reshape(1, C2))


def _dft_constants(L):
    N = 2 * L
    N1 = N // FFT_MINOR
    nh = N1 // 2
    k1h = N1 // 2 + 1
    k1p = -(-k1h // 8) * 8
    k1 = np.arange(k1h)[:, None]

    def stage_a(nn):
        th = 2 * np.pi * np.arange(nn)[None, :] * k1 / N1
        fa = np.zeros((2 * k1p, nn))
        fa[:k1h] = np.cos(th)
        fa[k1p:k1p + k1h] = -np.sin(th)
        return fa

    n2 = np.arange(FFT_MINOR)
    m1 = np.zeros((k1h, 2 * FFT_MINOR, 2 * FFT_MINOR))
    m2 = np.zeros_like(m1)
    for a in range(k1h):
        ph = -2 * np.pi * (n2[None, :] * a / N + n2[None, :] * n2[:, None] / FFT_MINOR)
        cr, ci = np.cos(ph), np.sin(ph)
        m1[a] = np.block([[cr, -ci], [ci, cr]])
        m2[a] = np.block([[cr.T, ci.T], [-ci.T, cr.T]])
    kk = np.arange(k1h)[None, :]
    cw = np.where((kk == 0) | (kk == N1 // 2), 1.0, 2.0) / N
    th = 2 * np.pi * np.arange(nh)[:, None] * kk / N1
    g = np.zeros((nh, 2 * k1p))
    g[:, :k1h] = cw * np.cos(th)
    g[:, k1p:k1p + k1h] = -cw * np.sin(th)
    as_bf16 = lambda a: jnp.asarray(a, F32).astype(BF16)
    return dict(nh=nh, n1=N1, k1h=k1h, k1p=k1p, fa_data=as_bf16(stage_a(nh)), fa_filt=as_bf16(stage_a(N1)),
                m1=as_bf16(m1), m2=as_bf16(m2), g=as_bf16(g))


def _lmat_kernel(f_ref, x_ref, o_ref):
    o_ref[...] = jnp.dot(f_ref[...], x_ref[...].astype(BF16), preferred_element_type=F32).astype(o_ref.dtype)


def _lmat(fmat, x2d, out_dtype=BF16, tn=8192):
    M, K = fmat.shape
    ncols = x2d.shape[1]
    tn = min(tn, ncols)
    return pl.pallas_call(
        _lmat_kernel,
        out_shape=jax.ShapeDtypeStruct((M, ncols), out_dtype),
        grid=(ncols // tn,),
        in_specs=[pl.BlockSpec((M, K), lambda j: (0, 0)),
                  pl.BlockSpec((K, tn), lambda j: (0, j))],
        out_specs=pl.BlockSpec((M, tn), lambda j: (0, j)),
        compiler_params=_cparams(("parallel",)),
        name="dft_stage_a",
    )(fmat, x2d)


def _lmat_gate_kernel(g_ref, z_ref, gate_ref, o_ref):
    o_ref[...] = jnp.dot(g_ref[...], z_ref[...], preferred_element_type=F32) * gate_ref[...]


def _lmat_gate(gmat, z2d, gate2d, tn=8192):
    M, K = gmat.shape
    ncols = z2d.shape[1]
    tn = min(tn, ncols)
    return pl.pallas_call(
        _lmat_gate_kernel,
        out_shape=jax.ShapeDtypeStruct((M, ncols), F32),
        grid=(ncols // tn,),
        in_specs=[pl.BlockSpec((M, K), lambda j: (0, 0)),
                  pl.BlockSpec((K, tn), lambda j: (0, j)),
                  pl.BlockSpec((M, tn), lambda j: (0, j))],
        out_specs=pl.BlockSpec((M, tn), lambda j: (0, j)),
        compiler_params=_cparams(("parallel",)),
        name="dft_stage_a_inv",
    )(gmat, z2d, gate2d)


def _filter_spectrum_kernel(m1_ref, a_ref, o_ref):
    x = jnp.concatenate([a_ref[0, 0], a_ref[1, 0]], axis=0)
    o_ref[0] = jnp.dot(m1_ref[0], x, preferred_element_type=F32).astype(o_ref.dtype)


def _filter_spectrum(m1, a4, k1h):
    _, k1p, n2, C = a4.shape
    return pl.pallas_call(
        _filter_spectrum_kernel,
        out_shape=jax.ShapeDtypeStruct((k1p, 2 * n2, C), BF16),
        grid=(k1p,),
        in_specs=[pl.BlockSpec((1, 2 * n2, 2 * n2), lambda a: (jnp.minimum(a, k1h - 1), 0, 0)),
                  pl.BlockSpec((2, 1, n2, C), lambda a: (0, a, 0, 0))],
        out_specs=pl.BlockSpec((1, 2 * n2, C), lambda a: (a, 0, 0)),
        compiler_params=_cparams(("parallel",)),
        name="filter_spectrum",
    )(m1, a4)


def _middle_kernel(m1_ref, m2_ref, a_ref, kf_ref, o_ref):
    n2 = a_ref.shape[2]
    x = jnp.concatenate([a_ref[0, 0], a_ref[1, 0]], axis=0)
    s = jnp.dot(m1_ref[0], x, preferred_element_type=F32)
    sr, si = s[:n2], s[n2:]
    kf = kf_ref[0].astype(F32)
    kr, ki = kf[:n2], kf[n2:]
    y = jnp.concatenate([sr * kr - si * ki, sr * ki + si * kr], axis=0).astype(BF16)
    z = jnp.dot(m2_ref[0], y, preferred_element_type=F32).astype(o_ref.dtype)
    o_ref[0, 0] = z[:n2]
    o_ref[1, 0] = z[n2:]


def _middle(m1, m2, a4, kf, order, k1h):
    _, k1p, n2, C = a4.shape
    mat = pl.BlockSpec((1, 2 * n2, 2 * n2), lambda a: (jnp.minimum(a, k1h - 1), 0, 0))
    return pl.pallas_call(
        _middle_kernel,
        out_shape=jax.ShapeDtypeStruct((2, k1p, n2, C), BF16),
        grid=(k1p,),
        in_specs=[mat, mat,
                  pl.BlockSpec((2, 1, n2, C), lambda a: (0, a, 0, 0)),
                  pl.BlockSpec((1, 2 * n2, C), lambda a: (a, 0, order))],
        out_specs=pl.BlockSpec((2, 1, n2, C), lambda a: (0, a, 0, 0)),
        compiler_params=_cparams(("parallel",)),
        name="dft_middle",
    )(m1, m2, a4, kf)


def _hyena_long_convs(v, g1, kfilt2d, L):
    C = v.shape[1]
    cs = _dft_constants(L)
    nh, k1h, k1p = cs["nh"], cs["k1h"], cs["k1p"]
    fa = _lmat(cs["fa_filt"], kfilt2d)
    kspec = _filter_spectrum(cs["m1"], fa.reshape(2, k1p, FFT_MINOR, 2 * C), k1h)
    a = _lmat(cs["fa_data"], v.reshape(nh, FFT_MINOR * C))
    z = _middle(cs["m1"], cs["m2"], a.reshape(2, k1p, FFT_MINOR, C), kspec, 0, k1h)
    y2d = _lmat_gate(cs["g"], z.reshape(2 * k1p, FFT_MINOR * C), g1.reshape(nh, FFT_MINOR * C))
    a = _lmat(cs["fa_data"], y2d)
    z = _middle(cs["m1"], cs["m2"], a.reshape(2, k1p, FFT_MINOR, C), kspec, 1, k1h)
    return _lmat(cs["g"], z.reshape(2 * k1p, FFT_MINOR * C), F32).reshape(L, C)


def _out_kernel(yh_ref, gate_ref, ya_ref, wh_ref, wa_ref, ng_ref, x_ref, g_ref, b_ref, o_ref):
    yh = yh_ref[...] * gate_ref[...]
    yh = yh * lax.rsqrt(jnp.mean(yh * yh, axis=-1, keepdims=True) + LN_EPS) * ng_ref[...]
    acc = jnp.dot(yh.astype(BF16), wh_ref[...], preferred_element_type=F32)
    acc = acc + jnp.dot(ya_ref[...], wa_ref[...], preferred_element_type=F32)
    o_ref[...] = _layer_norm(ALPHA * x_ref[...] + acc, g_ref[...], b_ref[...])


def _out_ln(y_hy, z, gate_block, y_at, w_out, norm_g, x1, g, b, tm=512):
    L, D = x1.shape
    ch, ca = y_hy.shape[1], y_at.shape[1]
    tm = min(tm, L)
    row = lambda c: pl.BlockSpec((tm, c), lambda i: (i, 0))
    full = lambda r, c: pl.BlockSpec((r, c), lambda i: (0, 0))
    return pl.pallas_call(
        _out_kernel,
        out_shape=jax.ShapeDtypeStruct((L, D), F32),
        grid=(L // tm,),
        in_specs=[row(ch), pl.BlockSpec((tm, ch), lambda i: (i, gate_block)), row(ca),
                  pl.BlockSpec((ch, D), lambda i: (0, 0)),
                  pl.BlockSpec((ca, D), lambda i: (1, 0)),
                  full(1, ch), row(D), full(1, D), full(1, D)],
        out_specs=row(D),
        compiler_params=_cparams(("parallel",)),
        name="out_ln",
    )(y_hy, z, y_at, w_out, w_out, norm_g, x1, g, b)


def kernel(x, ffn1_w_gate, ffn1_w_up, ffn1_w_down, ln1_g, ln1_b, w_in, hyena_conv_w, hyena_conv_b, filt_w1, filt_b1, filt_w2, filt_b2, filt_w3, filt_b3, filt_freq, filt_w_out, hyena_bias, hyena_norm_g, lambda_q1, lambda_k1, lambda_q2, lambda_k2, subln_g, w_out, ln2_g, ln2_b, ffn2_w_gate, ffn2_w_up, ffn2_w_down, ln3_g, ln3_b):
    assert x.shape[0] == 1 and ffn1_w_gate.shape[0] == DEPTH
    L = x.shape[1]
    bf = lambda a: a.astype(BF16)
    row = lambda a: a.reshape(1, -1)
    h = x[0]
    s_hy = 3 * D_HYENA
    for i in range(DEPTH):
        h = _ffn_ln(h, _ffn_up_blocks(ffn1_w_gate[i]), _ffn_up_blocks(ffn1_w_up[i]), bf(ffn1_w_down[i]),
                    row(ln1_g[i]), row(ln1_b[i]))
        w_in_b = bf(w_in[i])
        p_hy = _proj(h, w_in_b[:, :s_hy], F32)
        q_t = _proj_q_t(h, w_in_b[:, s_hy:s_hy + D_QK].T)
        k_at = _proj_rope(h, w_in_b[:, s_hy + D_QK:s_hy + 2 * D_QK])
        v_t = _proj_v_t(h, w_in_b[:, s_hy + 2 * D_QK:].T, ATTN_TK)
        z = _short_conv(p_hy, hyena_conv_w[i], row(hyena_conv_b[i]))
        kfilt = _hyena_filters(L, filt_w1[i], filt_b1[i], filt_w2[i], filt_b2[i], filt_w3[i], filt_b3[i],
                               filt_freq[i], filt_w_out[i], hyena_bias[i])
        y_hy = _hyena_long_convs(z[:, :D_HYENA], z[:, D_HYENA:2 * D_HYENA], kfilt, L)
        y_at = _diff_attention(q_t, k_at, v_t, row(lambda_q1[i]), row(lambda_k1[i]), row(lambda_q2[i]),
                               row(lambda_k2[i]), row(subln_g[i]))
        h = _out_ln(y_hy, z, 2, y_at, bf(w_out[i]), row(hyena_norm_g[i]), h, row(ln2_g[i]), row(ln2_b[i]))
        h = _ffn_ln(h, _ffn_up_blocks(ffn2_w_gate[i]), _ffn_up_blocks(ffn2_w_up[i]), bf(ffn2_w_down[i]),
                    row(ln3_g[i]), row(ln3_b[i]))
    return h[None]
```

```python
import functools
import math

import numpy as np
import jax
import jax.numpy as jnp
from jax import lax
from jax.experimental import pallas as pl
from jax.experimental.pallas import tpu as pltpu

F32 = jnp.float32
BF16 = jnp.bfloat16

D_HYENA = 1024
N_HEADS = 8
HEAD_DIM = 64
VAL_DIM = 128
D_QK = 2 * N_HEADS * HEAD_DIM
ROT_DIM = 16
ROPE_THETA = 500000.0
FILTER_EMB = 33
FILTER_HIDDEN = 64
DECAY_TARGET = 1e-2
MIN_DECAY = math.log(DECAY_TARGET) / 1.5
MAX_DECAY = math.log(DECAY_TARGET) / 0.3
LN_EPS = 1e-5
DEPTH = 1
ALPHA = (2.0 * DEPTH) ** 0.25
LAMBDA_INIT = 0.8 - 0.6 * math.exp(-0.3 * 0)
LOG2E = math.log2(math.e)

LANES = 128
FFT_MINOR = 128
VMEM_LIMIT = 56 * 1024 * 1024
ATTN_TQ = 512
ATTN_TK = 512
FIXED_SHIFT_MIN_DENOM = 2.0 ** -60


def _cparams(sem):
    return pltpu.CompilerParams(dimension_semantics=sem, vmem_limit_bytes=VMEM_LIMIT)


def _layer_norm(y, g, b):
    mu = jnp.mean(y, axis=-1, keepdims=True)
    d = y - mu
    var = jnp.mean(d * d, axis=-1, keepdims=True)
    return d * lax.rsqrt(var + LN_EPS) * g + b


def _ffn_kernel(x_ref, wg_ref, wu_ref, wd_ref, g_ref, b_ref, o_ref, xb_ref, acc_ref, h_ref, *, nf):
    f = pl.program_id(1)

    def up(slot):
        xb = xb_ref[...]
        hg = jnp.dot(xb, wg_ref[...], preferred_element_type=F32)
        hu = jnp.dot(xb, wu_ref[...], preferred_element_type=F32)
        h_ref[slot] = (hg * jax.nn.sigmoid(hg) * hu).astype(BF16)

    def down(slot):
        acc_ref[...] += jnp.dot(h_ref[slot], wd_ref[...], preferred_element_type=F32)

    @pl.when(f == 0)
    def _():
        xb_ref[...] = x_ref[...].astype(BF16)
        acc_ref[...] = jnp.zeros_like(acc_ref)
        up(0)

    @pl.when(jnp.logical_and(f > 0, f < nf))
    def _():
        down((f - 1) % 2)
        up(f % 2)

    @pl.when(f == nf)
    def _():
        down((nf - 1) % 2)
        y = ALPHA * x_ref[...] + 0.5 * acc_ref[...]
        o_ref[...] = _layer_norm(y, g_ref[...], b_ref[...])


def _ffn_ln(x, wg, wu, wd, g, b, tm=1024, tf=512):
    L, D = x.shape
    nf = wg.shape[1] // tf
    tm = min(tm, L)
    up_w = pl.BlockSpec((D, tf), lambda i, f: (0, jnp.minimum(f, nf - 1)))
    once = dict(pipeline_mode=pl.Buffered(1))
    return pl.pallas_call(
        functools.partial(_ffn_kernel, nf=nf),
        out_shape=jax.ShapeDtypeStruct((L, D), F32),
        grid=(L // tm, nf + 1),
        in_specs=[
            pl.BlockSpec((tm, D), lambda i, f: (i, 0), **once),
            up_w, up_w,
            pl.BlockSpec((tf, D), lambda i, f: (jnp.maximum(f - 1, 0), 0)),
            pl.BlockSpec((1, D), lambda i, f: (0, 0)),
            pl.BlockSpec((1, D), lambda i, f: (0, 0)),
        ],
        out_specs=pl.BlockSpec((tm, D), lambda i, f: (i, 0), **once),
        scratch_shapes=[pltpu.VMEM((tm, D), BF16), pltpu.VMEM((tm, D), F32), pltpu.VMEM((2, tm, tf), BF16)],
        compiler_params=_cparams(("parallel", "arbitrary")),
        name="ffn_ln",
    )(x, wg, wu, wd, g, b)


def _proj_kernel(x_ref, w_ref, o_ref, xb_ref):
    @pl.when(pl.program_id(1) == 0)
    def _():
        xb_ref[...] = x_ref[...].astype(BF16)

    o_ref[...] = jnp.dot(xb_ref[...], w_ref[...], preferred_element_type=F32).astype(o_ref.dtype)


def _proj(x, w, out_dtype, tm=1024, tn=1024):
    L, D = x.shape
    N = w.shape[1]
    tm = min(tm, L)
    return pl.pallas_call(
        _proj_kernel,
        out_shape=jax.ShapeDtypeStruct((L, N), out_dtype),
        grid=(L // tm, N // tn),
        in_specs=[pl.BlockSpec((tm, D), lambda i, j: (i, 0)),
                  pl.BlockSpec((D, tn), lambda i, j: (0, j))],
        out_specs=pl.BlockSpec((tm, tn), lambda i, j: (i, j)),
        scratch_shapes=[pltpu.VMEM((tm, D), BF16)],
        compiler_params=_cparams(("parallel", "arbitrary")),
        name="proj",
    )(x, w)


def _rope_tables(L, scale):
    half = ROT_DIM // 2
    inv = ROPE_THETA ** (-np.arange(0, ROT_DIM, 2, dtype=np.float64) / ROT_DIM)
    pos = np.arange(L, dtype=np.float64)[:, None]
    d = np.arange(LANES) % HEAD_DIM
    ang = pos * inv[d % half][None, :]
    c = np.where(d[None, :] < ROT_DIM, np.cos(ang), 1.0) * scale
    s1 = np.where(d[None, :] < half, -np.sin(ang), 0.0) * scale
    s2 = np.where((d[None, :] >= half) & (d[None, :] < ROT_DIM), np.sin(ang), 0.0) * scale
    return c, s1, s2


def _proj_rope_kernel(x_ref, w_ref, c_ref, s1_ref, s2_ref, o_ref, xb_ref, *, tn):
    @pl.when(pl.program_id(1) == 0)
    def _():
        xb_ref[...] = x_ref[...].astype(BF16)

    p = jnp.dot(xb_ref[...], w_ref[...], preferred_element_type=F32)
    c, s1, s2 = c_ref[...], s1_ref[...], s2_ref[...]
    shift = ROT_DIM // 2
    for h in range(tn // LANES):
        ph = p[:, h * LANES:(h + 1) * LANES]
        r = ph * c + pltpu.roll(ph, LANES - shift, 1) * s1 + pltpu.roll(ph, shift, 1) * s2
        o_ref[:, h * LANES:(h + 1) * LANES] = r.astype(o_ref.dtype)


def _proj_rope(x, w, tm=1024, tn=1024):
    L, D = x.shape
    N = w.shape[1]
    tm = min(tm, L)
    tabs = [jnp.asarray(t, F32) for t in _rope_tables(L, 1.0)]
    tab = pl.BlockSpec((tm, LANES), lambda i, j: (i, 0))
    return pl.pallas_call(
        functools.partial(_proj_rope_kernel, tn=tn),
        out_shape=jax.ShapeDtypeStruct((L, N), BF16),
        grid=(L // tm, N // tn),
        in_specs=[pl.BlockSpec((tm, D), lambda i, j: (i, 0)),
                  pl.BlockSpec((D, tn), lambda i, j: (0, j)),
                  tab, tab, tab],
        out_specs=pl.BlockSpec((tm, tn), lambda i, j: (i, j)),
        scratch_shapes=[pltpu.VMEM((tm, D), BF16)],
        compiler_params=_cparams(("parallel", "arbitrary")),
        name="proj_rope",
    )(x, w, *tabs)


def _projT_kernel(x_ref, wt_ref, *rest, rope, tn, tk):
    if rope:
        c_ref, s1_ref, s2_ref, o_ref, xb_ref = rest
    else:
        o_ref, xb_ref = rest

    @pl.when(pl.program_id(1) == 0)
    def _():
        xb_ref[...] = x_ref[...].astype(BF16)

    pt = lax.dot_general(wt_ref[...], xb_ref[...], (((1,), (1,)), ((), ())), preferred_element_type=F32)
    if rope:
        c, s1, s2 = c_ref[...], s1_ref[...], s2_ref[...]
        shift = ROT_DIM // 2
        for h in range(tn // LANES):
            ph = pt[h * LANES:(h + 1) * LANES, :]
            r = ph * c + pltpu.roll(ph, LANES - shift, 0) * s1 + pltpu.roll(ph, shift, 0) * s2
            o_ref[h * LANES:(h + 1) * LANES, :] = r.astype(o_ref.dtype)
    else:
        for s in range(pt.shape[1] // tk):
            o_ref[s] = pt[:, s * tk:(s + 1) * tk].astype(o_ref.dtype)


def _proj_q_t(x, wt, tm=1024, tn=512):
    L, D = x.shape
    N = wt.shape[0]
    tm = min(tm, L)
    tabs = [jnp.asarray(t.T, F32) for t in _rope_tables(L, HEAD_DIM ** -0.5 * LOG2E)]
    tab = pl.BlockSpec((LANES, tm), lambda i, j: (0, i))
    return pl.pallas_call(
        functools.partial(_projT_kernel, rope=True, tn=tn, tk=None),
        out_shape=jax.ShapeDtypeStruct((N, L), BF16),
        grid=(L // tm, N // tn),
        in_specs=[pl.BlockSpec((tm, D), lambda i, j: (i, 0)),
                  pl.BlockSpec((tn, D), lambda i, j: (j, 0)),
                  tab, tab, tab],
        out_specs=pl.BlockSpec((tn, tm), lambda i, j: (j, i)),
        scratch_shapes=[pltpu.VMEM((tm, D), BF16)],
        compiler_params=_cparams(("parallel", "arbitrary")),
        name="proj_q_t",
    )(x, wt, *tabs)


def _proj_v_t(x, wt, tk, tm=1024, tn=512):
    L, D = x.shape
    N = wt.shape[0]
    tm = min(tm, L)
    tk = min(tk, L)
    per = tm // tk
    return pl.pallas_call(
        functools.partial(_projT_kernel, rope=False, tn=tn, tk=tk),
        out_shape=jax.ShapeDtypeStruct((L // tk, N, tk), BF16),
        grid=(L // tm, N // tn),
        in_specs=[pl.BlockSpec((tm, D), lambda i, j: (i, 0)),
                  pl.BlockSpec((tn, D), lambda i, j: (j, 0))],
        out_specs=pl.BlockSpec((per, tn, tk), lambda i, j: (i, j, 0)),
        scratch_shapes=[pltpu.VMEM((tm, D), BF16)],
        compiler_params=_cparams(("parallel", "arbitrary")),
        name="proj_v_t",
    )(x, wt)


def _attn_kernel(qt_ref, k_ref, vt_ref, lq1_ref, lk1_ref, lq2_ref, lk2_ref, g_ref, o_ref,
                 acc_ref, l_ref, kn_ref, sa_ref, sb_ref, pa_ref, pb_ref, *, tk, nk):
    qt = qt_ref[...].astype(F32)
    tq = qt.shape[1]
    row = lax.broadcasted_iota(jnp.int32, qt.shape, 0)
    qs = (jnp.where(row < HEAD_DIM, qt, 0.0).astype(BF16),
          jnp.where(row >= HEAD_DIM, qt, 0.0).astype(BF16))
    zero = jnp.zeros((1, tq), F32)

    def k_chunk(j):
        return k_ref[pl.ds(pl.multiple_of(j * tk, tk), tk), :]

    @pl.when(pl.program_id(1) == 0)
    def _():
        def kbody(c, carry):
            kb = k_chunk(c).astype(F32)
            sq = kb * kb
            lane = lax.broadcasted_iota(jnp.int32, sq.shape, 1)
            n0 = jnp.sum(jnp.where(lane < HEAD_DIM, sq, 0.0), axis=1, keepdims=True)
            n1 = jnp.sum(jnp.where(lane >= HEAD_DIM, sq, 0.0), axis=1, keepdims=True)
            return (jnp.maximum(carry[0], jnp.max(n0, axis=0, keepdims=True)),
                    jnp.maximum(carry[1], jnp.max(n1, axis=0, keepdims=True)))

        z = jnp.zeros((1, 1), F32)
        k0, k1 = lax.fori_loop(0, nk, kbody, (z, z))
        kn_ref[0] = jnp.broadcast_to(k0, (1, LANES))
        kn_ref[1] = jnp.broadcast_to(k1, (1, LANES))

    q2 = qt * qt
    qn = (jnp.sum(q2[:HEAD_DIM], axis=0, keepdims=True), jnp.sum(q2[HEAD_DIM:], axis=0, keepdims=True))
    mb = tuple(jnp.sqrt(qn[mi] * kn_ref[mi][:, 0:1]) for mi in range(2))
    acc_ref[...] = jnp.zeros_like(acc_ref)

    def fixed_probs(j, p_ref):
        kb = k_chunk(j)
        sums = []
        for mi in range(2):
            pt = jnp.exp2(jnp.dot(kb, qs[mi], preferred_element_type=F32) - mb[mi])
            p_ref[mi] = pt.astype(BF16)
            sums.append(jnp.sum(pt, axis=0, keepdims=True))
        return tuple(sums)

    def fixed_accumulate(j, p_ref):
        vb = vt_ref[j]
        for mi in range(2):
            acc_ref[mi] += jnp.dot(vb, p_ref[mi], preferred_element_type=F32)

    def fixed_body(jj, carry):
        ls, la = carry
        j = 2 * jj
        lb = fixed_probs(j + 1, pb_ref)
        fixed_accumulate(j, pa_ref)
        ls = (ls[0] + la[0] + lb[0], ls[1] + la[1] + lb[1])
        la = fixed_probs(j + 2, pa_ref)
        fixed_accumulate(j + 1, pb_ref)
        return ls, la

    ls, la = lax.fori_loop(0, nk // 2 - 1, fixed_body, ((zero, zero), fixed_probs(0, pa_ref)))
    lb = fixed_probs(nk - 1, pb_ref)
    fixed_accumulate(nk - 2, pa_ref)
    fixed_accumulate(nk - 1, pb_ref)
    ls = (ls[0] + la[0] + lb[0], ls[1] + la[1] + lb[1])
    l_ref[0] = ls[0]
    l_ref[1] = ls[1]
    trusted = jnp.min(jnp.minimum(ls[0], ls[1])) >= FIXED_SHIFT_MIN_DENOM

    @pl.when(jnp.logical_not(trusted))
    def _():
        acc_ref[...] = jnp.zeros_like(acc_ref)
        neg = jnp.full((1, tq), -jnp.inf, F32)

        def scores(j, s_ref):
            kb = k_chunk(j)
            mx = []
            for mi in range(2):
                st = jnp.dot(kb, qs[mi], preferred_element_type=F32)
                s_ref[mi] = st
                mx.append(jnp.max(st, axis=0, keepdims=True))
            return tuple(mx)

        def consume(j, s_ref, mx, ms, lr):
            vb = vt_ref[j]
            new_m, new_l = [], []
            for mi in range(2):
                m_next = jnp.maximum(ms[mi], mx[mi])
                alpha = jnp.exp2(ms[mi] - m_next)
                pt = jnp.exp2(s_ref[mi] - m_next)
                new_l.append(alpha * lr[mi] + jnp.sum(pt, axis=0, keepdims=True))
                new_m.append(m_next)
                acc_ref[mi] = acc_ref[mi] * alpha + jnp.dot(vb, pt.astype(BF16), preferred_element_type=F32)
            return tuple(new_m), tuple(new_l)

        def body(jj, carry):
            ms, lr, mxa = carry
            j = 2 * jj
            mxb = scores(j + 1, sb_ref)
            ms, lr = consume(j, sa_ref, mxa, ms, lr)
            mxa = scores(jnp.minimum(j + 2, nk - 1), sa_ref)
            ms, lr = consume(j + 1, sb_ref, mxb, ms, lr)
            return ms, lr, mxa

        _, lr, _ = lax.fori_loop(0, nk // 2, body, ((neg, neg), (zero, zero), scores(0, sa_ref)))
        l_ref[0] = lr[0]
        l_ref[1] = lr[1]

    lam = (jnp.exp(jnp.sum(lq1_ref[...] * lk1_ref[...], axis=-1, keepdims=True))
           - jnp.exp(jnp.sum(lq2_ref[...] * lk2_ref[...], axis=-1, keepdims=True)) + LAMBDA_INIT)
    ot = acc_ref[0] / l_ref[0] - lam * (acc_ref[1] / l_ref[1])
    ot = ot * lax.rsqrt(jnp.mean(ot * ot, axis=0, keepdims=True) + LN_EPS)
    o_ref[...] = (ot.T * (g_ref[...] * (1.0 - LAMBDA_INIT))).astype(o_ref.dtype)


def _diff_attention(qt, k, vt, lq1, lk1, lq2, lk2, subln_g, tq=ATTN_TQ):
    L = k.shape[0]
    nk, _, tk = vt.shape
    assert nk % 2 == 0, "the kv loop handles chunks in pairs"
    tq = min(tq, L)
    vec = pl.BlockSpec((1, HEAD_DIM), lambda h, i: (0, 0))
    return pl.pallas_call(
        functools.partial(_attn_kernel, tk=tk, nk=nk),
        out_shape=jax.ShapeDtypeStruct((L, N_HEADS * VAL_DIM), BF16),
        grid=(N_HEADS, L // tq),
        in_specs=[pl.BlockSpec((LANES, tq), lambda h, i: (h, i)),
                  pl.BlockSpec((L, LANES), lambda h, i: (0, h)),
                  pl.BlockSpec((nk, VAL_DIM, tk), lambda h, i: (0, h, 0)),
                  vec, vec, vec, vec,
                  pl.BlockSpec((1, VAL_DIM), lambda h, i: (0, 0))],
        out_specs=pl.BlockSpec((tq, VAL_DIM), lambda h, i: (i, h)),
        scratch_shapes=[pltpu.VMEM((2, VAL_DIM, tq), F32), pltpu.VMEM((2, 1, tq), F32),
                        pltpu.VMEM((2, 1, LANES), F32),
                        pltpu.VMEM((2, tk, tq), F32), pltpu.VMEM((2, tk, tq), F32),
                        pltpu.VMEM((2, tk, tq), BF16), pltpu.VMEM((2, tk, tq), BF16)],
        compiler_params=_cparams(("parallel", "arbitrary")),
        name="diff_attention",
    )(qt, k, vt, lq1, lk1, lq2, lk2, subln_g)


def _conv_kernel(x_ref, w_ref, b_ref, o_ref, *, rows, nchunks):
    w0 = w_ref[0:1, :]
    w1 = w_ref[1:2, :]
    w2 = w_ref[2:3, :]
    b = b_ref[...]
    last = nchunks * rows - 8

    def body(i, carry):
        r0 = pl.multiple_of(i * rows, rows)
        x0 = x_ref[pl.ds(r0, rows), :]
        pr = x_ref[pl.ds(pl.multiple_of(jnp.maximum(r0 - 8, 0), 8), 8), :]
        nx = x_ref[pl.ds(pl.multiple_of(jnp.minimum(r0 + rows, last), 8), 8), :]
        prev_row = jnp.where(i > 0, pr[7:8, :], 0.0)
        next_row = jnp.where(i < nchunks - 1, nx[0:1, :], 0.0)
        row = lax.broadcasted_iota(jnp.int32, x0.shape, 0)
        xm = jnp.where(row == 0, prev_row, pltpu.roll(x0, 1, 0))
        xp = jnp.where(row == rows - 1, next_row, pltpu.roll(x0, rows - 1, 0))
        o_ref[pl.ds(r0, rows), :] = xm * w0 + x0 * w1 + xp * w2 + b
        return carry

    lax.fori_loop(0, nchunks, body, 0)


def _short_conv(p, w, b, tn=256, rows=512):
    L, C = p.shape
    rows = min(rows, L)
    return pl.pallas_call(
        functools.partial(_conv_kernel, rows=rows, nchunks=L // rows),
        out_shape=jax.ShapeDtypeStruct((L, C), F32),
        grid=(C // tn,),
        in_specs=[pl.BlockSpec((L, tn), lambda j: (0, j)),
                  pl.BlockSpec((3, tn), lambda j: (0, j)),
                  pl.BlockSpec((1, tn), lambda j: (0, j))],
        out_specs=pl.BlockSpec((L, tn), lambda j: (0, j)),
        compiler_params=_cparams(("parallel",)),
        name="short_conv",
    )(p, w, b)


def _split_bf16(a):
    hi = a.astype(BF16)
    return hi, (a - hi.astype(F32)).astype(BF16)


def _filter_kernel(z_ref, w1_ref, b1_ref, w2_ref, b2_ref, w3_ref, b3_ref, fr_ref, wfh_ref, wfl_ref,
                   wbh_ref, wbl_ref, dl_ref, bias_ref, o_ref, *, L, nh):
    hp = lax.Precision.HIGHEST
    fr = fr_ref[...]
    h = jnp.sin(fr * (jnp.dot(z_ref[...], w1_ref[...], precision=hp, preferred_element_type=F32) + b1_ref[...]))
    h = jnp.sin(fr * (jnp.dot(h, w2_ref[...], precision=hp, preferred_element_type=F32) + b2_ref[...]))
    h = jnp.sin(fr * (jnp.dot(h, w3_ref[...], precision=hp, preferred_element_type=F32) + b3_ref[...]))
    hh, hl = _split_bf16(h)

    def out_proj(wh_ref, wl_ref):
        wh = wh_ref[...]
        return (jnp.dot(hh, wh, preferred_element_type=F32) + jnp.dot(hl, wh, preferred_element_type=F32)
                + jnp.dot(hh, wl_ref[...], preferred_element_type=F32))

    rb = z_ref.shape[0]
    nq = rb // nh
    c2 = dl_ref.shape[1]
    row = lax.broadcasted_iota(jnp.int32, (rb, 1), 0)
    assert nh & (nh - 1) == 0
    r = jnp.bitwise_and(row, nh - 1)
    q_idx = lax.shift_right_logical(row, nh.bit_length() - 1)
    n = FFT_MINOR * r + nq * pl.program_id(0) + q_idx
    step = 1.0 / (L - 1)
    dl = dl_ref[...]
    kf = out_proj(wfh_ref, wfl_ref) * jnp.exp(-(n.astype(F32) * step) * dl)
    kf = (kf + jnp.where(n == 0, bias_ref[...], 0.0)).astype(o_ref.dtype)
    kb = out_proj(wbh_ref, wbl_ref) * jnp.exp(-((L - n).astype(F32) * step) * dl)
    kb = jnp.where(n == 0, 0.0, kb).astype(o_ref.dtype)
    for q in range(nq):
        o_ref[0:nh, q * c2:(q + 1) * c2] = kf[q * nh:(q + 1) * nh]
        o_ref[nh:2 * nh, q * c2:(q + 1) * c2] = kb[q * nh:(q + 1) * nh]


def _filter_features(L, nq):
    half = LANES // 2
    nh = L // FFT_MINOR
    bands = (FILTER_EMB - 1) // 2
    f = np.linspace(1e-4, bands - 1, bands)
    z = np.zeros((L, LANES))
    step, q, r = np.meshgrid(np.arange(FFT_MINOR // nq), np.arange(nq), np.arange(nh), indexing="ij")
    n = (FFT_MINOR * r + nq * step + q).reshape(-1).astype(np.float64)
    for off, pos in ((0, n), (half, L - n)):
        w = 2.0 * np.pi * pos / L
        z[:, off] = pos / (L - 1)
        z[:, off + 1:off + 1 + bands] = np.cos(f[None, :] * w[:, None])
        z[:, off + 1 + bands:off + 1 + 2 * bands] = -np.sin(f[None, :] * w[:, None])
    return jnp.asarray(z, F32)


def _pad_to(a, shape):
    return jnp.pad(a, [(0, s - d) for d, s in zip(a.shape, shape)])


def _block_diag2(a):
    half = LANES // 2
    p = _pad_to(a, (half, half))
    z = jnp.zeros_like(p)
    return jnp.concatenate([jnp.concatenate([p, z], axis=1), jnp.concatenate([z, p], axis=1)], axis=0)


def _hyena_filters(L, fw1, fb1, fw2, fb2, fw3, fb3, ffreq, fw_out, bias, nq=8):
    H = LANES
    half = H // 2
    C2 = 2 * D_HYENA
    nh = L // FFT_MINOR
    rb = nq * nh
    dup = lambda v: jnp.tile(_pad_to(v[None, :], (1, half)), (1, 2))
    wo = fw_out.reshape(FILTER_HIDDEN, 2, 2, D_HYENA)
    wf = _pad_to(wo[:, :, 0, :].reshape(FILTER_HIDDEN, C2), (H, C2))
    wb = jnp.concatenate([jnp.zeros((half, C2), F32), _pad_to(wo[:, :, 1, :].reshape(FILTER_HIDDEN, C2), (half, C2))])
    deltas = np.abs(np.linspace(MIN_DECAY, MAX_DECAY, D_HYENA))
    dl = jnp.asarray(np.tile(deltas, 2)[None, :], F32)
    full = lambda shape: pl.BlockSpec(shape, lambda i: (0,) * len(shape))
    return pl.pallas_call(
        functools.partial(_filter_kernel, L=L, nh=nh),
        out_shape=jax.ShapeDtypeStruct((2 * nh, FFT_MINOR * C2), BF16),
        grid=(L // rb,),
        in_specs=[pl.BlockSpec((rb, H), lambda i: (i, 0)),
                  full((H, H)), full((1, H)), full((H, H)), full((1, H)), full((H, H)), full((1, H)),
                  full((1, H)),
                  full((H, C2)), full((H, C2)), full((H, C2)), full((H, C2)),
                  full((1, C2)), full((1, C2))],
        out_specs=pl.BlockSpec((2 * nh, nq * C2), lambda i: (0, i)),
        compiler_params=_cparams(("parallel",)),
        name="hyena_filters",
    )(_filter_features(L, nq), _block_diag2(fw1), dup(fb1), _block_diag2(fw2), dup(fb2), _block_diag2(fw3),
      dup(fb3), dup(ffreq), *_split_bf16(wf), *_split_bf16(wb), dl, bias.reshape(1, C2))


def _dft_constants(L):
    N = 2 * L
    N1 = N // FFT_MINOR
    nh = N1 // 2
    k1h = N1 // 2 + 1
    k1p = -(-k1h // 8) * 8
    k1 = np.arange(k1h)[:, None]

    def stage_a(nn):
        th = 2 * np.pi * np.arange(nn)[None, :] * k1 / N1
        fa = np.zeros((2 * k1p, nn))
        fa[:k1h] = np.cos(th)
        fa[k1p:k1p + k1h] = -np.sin(th)
        return fa

    n2 = np.arange(FFT_MINOR)
    m1 = np.zeros((k1p, 2 * FFT_MINOR, 2 * FFT_MINOR))
    m2 = np.zeros_like(m1)
    for a in range(k1h):
        ph = -2 * np.pi * (n2[None, :] * a / N + n2[None, :] * n2[:, None] / FFT_MINOR)
        cr, ci = np.cos(ph), np.sin(ph)
        m1[a] = np.block([[cr, -ci], [ci, cr]])
        m2[a] = np.block([[cr.T, ci.T], [-ci.T, cr.T]])
    kk = np.arange(k1h)[None, :]
    cw = np.where((kk == 0) | (kk == N1 // 2), 1.0, 2.0) / N
    th = 2 * np.pi * np.arange(nh)[:, None] * kk / N1
    g = np.zeros((nh, 2 * k1p))
    g[:, :k1h] = cw * np.cos(th)
    g[:, k1p:k1p + k1h] = -cw * np.sin(th)
    as_bf16 = lambda a: jnp.asarray(a, F32).astype(BF16)
    return dict(nh=nh, n1=N1, k1h=k1h, k1p=k1p, fa_data=as_bf16(stage_a(nh)), fa_filt=as_bf16(stage_a(N1)),
                m1=as_bf16(m1), m2=as_bf16(m2), g=as_bf16(g))


def _lmat_kernel(f_ref, x_ref, o_ref):
    o_ref[...] = jnp.dot(f_ref[...], x_ref[...].astype(BF16), preferred_element_type=F32).astype(o_ref.dtype)


def _lmat(fmat, x2d, out_dtype=BF16, tn=8192):
    M, K = fmat.shape
    ncols = x2d.shape[1]
    tn = min(tn, ncols)
    return pl.pallas_call(
        _lmat_kernel,
        out_shape=jax.ShapeDtypeStruct((M, ncols), out_dtype),
        grid=(ncols // tn,),
        in_specs=[pl.BlockSpec((M, K), lambda j: (0, 0)),
                  pl.BlockSpec((K, tn), lambda j: (0, j))],
        out_specs=pl.BlockSpec((M, tn), lambda j: (0, j)),
        compiler_params=_cparams(("parallel",)),
        name="dft_stage_a",
    )(fmat, x2d)


def _lmat_gate_kernel(g_ref, z_ref, gate_ref, o_ref):
    o_ref[...] = jnp.dot(g_ref[...], z_ref[...], preferred_element_type=F32) * gate_ref[...]


def _lmat_gate(gmat, z2d, gate2d, tn=8192):
    M, K = gmat.shape
    ncols = z2d.shape[1]
    tn = min(tn, ncols)
    return pl.pallas_call(
        _lmat_gate_kernel,
        out_shape=jax.ShapeDtypeStruct((M, ncols), F32),
        grid=(ncols // tn,),
        in_specs=[pl.BlockSpec((M, K), lambda j: (0, 0)),
                  pl.BlockSpec((K, tn), lambda j: (0, j)),
                  pl.BlockSpec((M, tn), lambda j: (0, j))],
        out_specs=pl.BlockSpec((M, tn), lambda j: (0, j)),
        compiler_params=_cparams(("parallel",)),
        name="dft_stage_a_inv",
    )(gmat, z2d, gate2d)


K1_PER_STEP = 4


def _filter_spectrum_kernel(m1_ref, a_ref, o_ref):
    for b in range(m1_ref.shape[0]):
        x = jnp.concatenate([a_ref[0, b], a_ref[1, b]], axis=0)
        o_ref[b] = jnp.dot(m1_ref[b], x, preferred_element_type=F32).astype(o_ref.dtype)


def _filter_spectrum(m1, a4):
    _, k1p, n2, C = a4.shape
    kb = K1_PER_STEP
    return pl.pallas_call(
        _filter_spectrum_kernel,
        out_shape=jax.ShapeDtypeStruct((k1p, 2 * n2, C), BF16),
        grid=(k1p // kb,),
        in_specs=[pl.BlockSpec((kb, 2 * n2, 2 * n2), lambda a: (a, 0, 0)),
                  pl.BlockSpec((2, kb, n2, C), lambda a: (0, a, 0, 0))],
        out_specs=pl.BlockSpec((kb, 2 * n2, C), lambda a: (a, 0, 0)),
        compiler_params=_cparams(("parallel",)),
        name="filter_spectrum",
    )(m1, a4)


def _middle_kernel(m1_ref, m2_ref, a_ref, kf_ref, o_ref):
    n2 = a_ref.shape[2]
    for b in range(m1_ref.shape[0]):
        x = jnp.concatenate([a_ref[0, b], a_ref[1, b]], axis=0)
        s = jnp.dot(m1_ref[b], x, preferred_element_type=F32)
        sr, si = s[:n2], s[n2:]
        kf = kf_ref[b].astype(F32)
        kr, ki = kf[:n2], kf[n2:]
        y = jnp.concatenate([sr * kr - si * ki, sr * ki + si * kr], axis=0).astype(BF16)
        z = jnp.dot(m2_ref[b], y, preferred_element_type=F32).astype(o_ref.dtype)
        o_ref[0, b] = z[:n2]
        o_ref[1, b] = z[n2:]


def _middle(m1, m2, a4, kf, order):
    _, k1p, n2, C = a4.shape
    kb = K1_PER_STEP
    mat = pl.BlockSpec((kb, 2 * n2, 2 * n2), lambda a: (a, 0, 0))
    return pl.pallas_call(
        _middle_kernel,
        out_shape=jax.ShapeDtypeStruct((2, k1p, n2, C), BF16),
        grid=(k1p // kb,),
        in_specs=[mat, mat,
                  pl.BlockSpec((2, kb, n2, C), lambda a: (0, a, 0, 0)),
                  pl.BlockSpec((kb, 2 * n2, C), lambda a: (a, 0, order))],
        out_specs=pl.BlockSpec((2, kb, n2, C), lambda a: (0, a, 0, 0)),
        compiler_params=_cparams(("parallel",)),
        name="dft_middle",
    )(m1, m2, a4, kf)


def _hyena_long_convs(v, g1, kfilt2d, L):
    C = v.shape[1]
    cs = _dft_constants(L)
    nh, k1p = cs["nh"], cs["k1p"]
    fa = _lmat(cs["fa_filt"], kfilt2d)
    kspec = _filter_spectrum(cs["m1"], fa.reshape(2, k1p, FFT_MINOR, 2 * C))
    a = _lmat(cs["fa_data"], v.reshape(nh, FFT_MINOR * C))
    z = _middle(cs["m1"], cs["m2"], a.reshape(2, k1p, FFT_MINOR, C), kspec, 0)
    y2d = _lmat_gate(cs["g"], z.reshape(2 * k1p, FFT_MINOR * C), g1.reshape(nh, FFT_MINOR * C))
    a = _lmat(cs["fa_data"], y2d)
    z = _middle(cs["m1"], cs["m2"], a.reshape(2, k1p, FFT_MINOR, C), kspec, 1)
    return _lmat(cs["g"], z.reshape(2 * k1p, FFT_MINOR * C), F32).reshape(L, C)


def _out_kernel(yh_ref, gate_ref, ya_ref, wh_ref, wa_ref, ng_ref, x_ref, g_ref, b_ref, o_ref):
    yh = yh_ref[...] * gate_ref[...]
    yh = yh * lax.rsqrt(jnp.mean(yh * yh, axis=-1, keepdims=True) + LN_EPS) * ng_ref[...]
    acc = jnp.dot(yh.astype(BF16), wh_ref[...], preferred_element_type=F32)
    acc = acc + jnp.dot(ya_ref[...], wa_ref[...], preferred_element_type=F32)
    o_ref[...] = _layer_norm(ALPHA * x_ref[...] + acc, g_ref[...], b_ref[...])


def _out_ln(y_hy, z, gate_block, y_at, w_out, norm_g, x1, g, b, tm=512):
    L, D = x1.shape
    ch, ca = y_hy.shape[1], y_at.shape[1]
    tm = min(tm, L)
    row = lambda c: pl.BlockSpec((tm, c), lambda i: (i, 0))
    full = lambda r, c: pl.BlockSpec((r, c), lambda i: (0, 0))
    return pl.pallas_call(
        _out_kernel,
        out_shape=jax.ShapeDtypeStruct((L, D), F32),
        grid=(L // tm,),
        in_specs=[row(ch), pl.BlockSpec((tm, ch), lambda i: (i, gate_block)), row(ca),
                  pl.BlockSpec((ch, D), lambda i: (0, 0)),
                  pl.BlockSpec((ca, D), lambda i: (1, 0)),
                  full(1, ch), row(D), full(1, D), full(1, D)],
        out_specs=row(D),
        compiler_params=_cparams(("parallel",)),
        name="out_ln",
    )(y_hy, z, y_at, w_out, w_out, norm_g, x1, g, b)


def kernel(x, ffn1_w_gate, ffn1_w_up, ffn1_w_down, ln1_g, ln1_b, w_in, hyena_conv_w, hyena_conv_b, filt_w1, filt_b1, filt_w2, filt_b2, filt_w3, filt_b3, filt_freq, filt_w_out, hyena_bias, hyena_norm_g, lambda_q1, lambda_k1, lambda_q2, lambda_k2, subln_g, w_out, ln2_g, ln2_b, ffn2_w_gate, ffn2_w_up, ffn2_w_down, ln3_g, ln3_b):
    assert x.shape[0] == 1 and ffn1_w_gate.shape[0] == DEPTH
    L = x.shape[1]
    bf = lambda a: a.astype(BF16)
    row = lambda a: a.reshape(1, -1)
    h = x[0]
    s_hy = 3 * D_HYENA
    for i in range(DEPTH):
        h = _ffn_ln(h, bf(ffn1_w_gate[i]), bf(ffn1_w_up[i]), bf(ffn1_w_down[i]), row(ln1_g[i]), row(ln1_b[i]))
        w_in_b = bf(w_in[i])
        p_hy = _proj(h, w_in_b[:, :s_hy], F32)
        q_t = _proj_q_t(h, w_in_b[:, s_hy:s_hy + D_QK].T)
        k_at = _proj_rope(h, w_in_b[:, s_hy + D_QK:s_hy + 2 * D_QK])
        v_t = _proj_v_t(h, w_in_b[:, s_hy + 2 * D_QK:].T, ATTN_TK)
        z = _short_conv(p_hy, hyena_conv_w[i], row(hyena_conv_b[i]))
        kfilt = _hyena_filters(L, filt_w1[i], filt_b1[i], filt_w2[i], filt_b2[i], filt_w3[i], filt_b3[i],
                               filt_freq[i], filt_w_out[i], hyena_bias[i])
        y_hy = _hyena_long_convs(z[:, :D_HYENA], z[:, D_HYENA:2 * D_HYENA], kfilt, L)
        y_at = _diff_attention(q_t, k_at, v_t, row(lambda_q1[i]), row(lambda_k1[i]), row(lambda_q2[i]),
                               row(lambda_k2[i]), row(subln_g[i]))
        h = _out_ln(y_hy, z, 2, y_at, bf(w_out[i]), row(hyena_norm_g[i]), h, row(ln2_g[i]), row(ln2_b[i]))
        h = _ffn_ln(h, bf(ffn2_w_gate[i]), bf(ffn2_w_up[i]), bf(ffn2_w_down[i]), row(ln3_g[i]), row(ln3_b[i]))
    return h[None]
```

```python
import functools
import math

import numpy as np
import jax
import jax.numpy as jnp
from jax import lax
from jax.experimental import pallas as pl
from jax.experimental.pallas import tpu as pltpu

F32 = jnp.float32
BF16 = jnp.bfloat16

D_HYENA = 1024
N_HEADS = 8
HEAD_DIM = 64
VAL_DIM = 128
D_QK = 2 * N_HEADS * HEAD_DIM
ROT_DIM = 16
ROPE_THETA = 500000.0
FILTER_EMB = 33
FILTER_HIDDEN = 64
DECAY_TARGET = 1e-2
MIN_DECAY = math.log(DECAY_TARGET) / 1.5
MAX_DECAY = math.log(DECAY_TARGET) / 0.3
LN_EPS = 1e-5
DEPTH = 1
ALPHA = (2.0 * DEPTH) ** 0.25
LAMBDA_INIT = 0.8 - 0.6 * math.exp(-0.3 * 0)
LOG2E = math.log2(math.e)

LANES = 128
BF16_SUBLANES = 16
FFT_MINOR = 128
VMEM_LIMIT = 56 * 1024 * 1024
ATTN_TQ = 512
ATTN_TK = 512
FIXED_SHIFT_MIN_DENOM = 2.0 ** -60


def _cparams(sem):
    return pltpu.CompilerParams(dimension_semantics=sem, vmem_limit_bytes=VMEM_LIMIT)


def _layer_norm(y, g, b):
    mu = jnp.mean(y, axis=-1, keepdims=True)
    d = y - mu
    var = jnp.mean(d * d, axis=-1, keepdims=True)
    return d * lax.rsqrt(var + LN_EPS) * g + b


def _ffn_kernel(x_ref, wg_ref, wu_ref, wd_ref, g_ref, b_ref, o_ref, xb_ref, acc_ref, h_ref, *, nf):
    f = pl.program_id(1)

    def up(slot):
        xb = xb_ref[...]
        hg = jnp.dot(xb, wg_ref[...], preferred_element_type=F32)
        hu = jnp.dot(xb, wu_ref[...], preferred_element_type=F32)
        h_ref[slot] = (hg * jax.nn.sigmoid(hg) * hu).astype(BF16)

    def down(slot):
        acc_ref[...] += jnp.dot(h_ref[slot], wd_ref[...], preferred_element_type=F32)

    @pl.when(f == 0)
    def _():
        xb_ref[...] = x_ref[...].astype(BF16)
        acc_ref[...] = jnp.zeros_like(acc_ref)
        up(0)

    @pl.when(jnp.logical_and(f > 0, f < nf))
    def _():
        down((f - 1) % 2)
        up(f % 2)

    @pl.when(f == nf)
    def _():
        down((nf - 1) % 2)
        y = ALPHA * x_ref[...] + 0.5 * acc_ref[...]
        o_ref[...] = _layer_norm(y, g_ref[...], b_ref[...])


def _ffn_ln(x, wg, wu, wd, g, b, tm=1024, tf=512):
    L, D = x.shape
    nf = wg.shape[1] // tf
    tm = min(tm, L)
    up_w = pl.BlockSpec((D, tf), lambda i, f: (0, jnp.minimum(f, nf - 1)))
    once = dict(pipeline_mode=pl.Buffered(1))
    return pl.pallas_call(
        functools.partial(_ffn_kernel, nf=nf),
        out_shape=jax.ShapeDtypeStruct((L, D), F32),
        grid=(L // tm, nf + 1),
        in_specs=[
            pl.BlockSpec((tm, D), lambda i, f: (i, 0), **once),
            up_w, up_w,
            pl.BlockSpec((tf, D), lambda i, f: (jnp.maximum(f - 1, 0), 0)),
            pl.BlockSpec((1, D), lambda i, f: (0, 0)),
            pl.BlockSpec((1, D), lambda i, f: (0, 0)),
        ],
        out_specs=pl.BlockSpec((tm, D), lambda i, f: (i, 0), **once),
        scratch_shapes=[pltpu.VMEM((tm, D), BF16), pltpu.VMEM((tm, D), F32), pltpu.VMEM((2, tm, tf), BF16)],
        compiler_params=_cparams(("parallel", "arbitrary")),
        name="ffn_ln",
    )(x, wg, wu, wd, g, b)


def _proj_kernel(x_ref, w_ref, o_ref, xb_ref):
    @pl.when(pl.program_id(1) == 0)
    def _():
        xb_ref[...] = x_ref[...].astype(BF16)

    o_ref[...] = jnp.dot(xb_ref[...], w_ref[...], preferred_element_type=F32).astype(o_ref.dtype)


def _proj(x, w, out_dtype, tm=1024, tn=1024):
    L, D = x.shape
    N = w.shape[1]
    tm = min(tm, L)
    return pl.pallas_call(
        _proj_kernel,
        out_shape=jax.ShapeDtypeStruct((L, N), out_dtype),
        grid=(L // tm, N // tn),
        in_specs=[pl.BlockSpec((tm, D), lambda i, j: (i, 0)),
                  pl.BlockSpec((D, tn), lambda i, j: (0, j))],
        out_specs=pl.BlockSpec((tm, tn), lambda i, j: (i, j)),
        scratch_shapes=[pltpu.VMEM((tm, D), BF16)],
        compiler_params=_cparams(("parallel", "arbitrary")),
        name="proj",
    )(x, w)


def _rope_tables(L, scale):
    half = ROT_DIM // 2
    inv = ROPE_THETA ** (-np.arange(0, ROT_DIM, 2, dtype=np.float64) / ROT_DIM)
    pos = np.arange(L, dtype=np.float64)[:, None]
    d = np.arange(LANES) % HEAD_DIM
    ang = pos * inv[d % half][None, :]
    c = np.where(d[None, :] < ROT_DIM, np.cos(ang), 1.0) * scale
    s1 = np.where(d[None, :] < half, -np.sin(ang), 0.0) * scale
    s2 = np.where((d[None, :] >= half) & (d[None, :] < ROT_DIM), np.sin(ang), 0.0) * scale
    return c, s1, s2


def _proj_rope_kernel(x_ref, w_ref, c_ref, s1_ref, s2_ref, o_ref, xb_ref, *, tn):
    @pl.when(pl.program_id(1) == 0)
    def _():
        xb_ref[...] = x_ref[...].astype(BF16)

    p = jnp.dot(xb_ref[...], w_ref[...], preferred_element_type=F32)
    c, s1, s2 = c_ref[...], s1_ref[...], s2_ref[...]
    shift = ROT_DIM // 2
    for h in range(tn // LANES):
        ph = p[:, h * LANES:(h + 1) * LANES]
        r = ph * c + pltpu.roll(ph, LANES - shift, 1) * s1 + pltpu.roll(ph, shift, 1) * s2
        o_ref[:, h * LANES:(h + 1) * LANES] = r.astype(o_ref.dtype)


def _proj_rope(x, w, tm=1024, tn=1024):
    L, D = x.shape
    N = w.shape[1]
    tm = min(tm, L)
    tabs = [jnp.asarray(t, F32) for t in _rope_tables(L, 1.0)]
    tab = pl.BlockSpec((tm, LANES), lambda i, j: (i, 0))
    return pl.pallas_call(
        functools.partial(_proj_rope_kernel, tn=tn),
        out_shape=jax.ShapeDtypeStruct((L, N), BF16),
        grid=(L // tm, N // tn),
        in_specs=[pl.BlockSpec((tm, D), lambda i, j: (i, 0)),
                  pl.BlockSpec((D, tn), lambda i, j: (0, j)),
                  tab, tab, tab],
        out_specs=pl.BlockSpec((tm, tn), lambda i, j: (i, j)),
        scratch_shapes=[pltpu.VMEM((tm, D), BF16)],
        compiler_params=_cparams(("parallel", "arbitrary")),
        name="proj_rope",
    )(x, w, *tabs)


def _projT_kernel(x_ref, wt_ref, *rest, rope, tn, tk):
    if rope:
        c_ref, s1_ref, s2_ref, o_ref, xb_ref = rest
    else:
        o_ref, xb_ref = rest

    @pl.when(pl.program_id(1) == 0)
    def _():
        xb_ref[...] = x_ref[...].astype(BF16)

    pt = lax.dot_general(wt_ref[...], xb_ref[...], (((1,), (1,)), ((), ())), preferred_element_type=F32)
    if rope:
        c, s1, s2 = c_ref[...], s1_ref[...], s2_ref[...]
        shift = ROT_DIM // 2
        for h in range(tn // LANES):
            ph = pt[h * LANES:(h + 1) * LANES, :]
            r = ph * c + pltpu.roll(ph, LANES - shift, 0) * s1 + pltpu.roll(ph, shift, 0) * s2
            o_ref[h * LANES:(h + 1) * LANES, :] = r.astype(o_ref.dtype)
    else:
        for s in range(pt.shape[1] // tk):
            o_ref[s] = pt[:, s * tk:(s + 1) * tk].astype(o_ref.dtype)


def _proj_q_t(x, wt, tm=1024, tn=512):
    L, D = x.shape
    N = wt.shape[0]
    tm = min(tm, L)
    tabs = [jnp.asarray(t.T, F32) for t in _rope_tables(L, HEAD_DIM ** -0.5 * LOG2E)]
    tab = pl.BlockSpec((LANES, tm), lambda i, j: (0, i))
    return pl.pallas_call(
        functools.partial(_projT_kernel, rope=True, tn=tn, tk=None),
        out_shape=jax.ShapeDtypeStruct((N, L), BF16),
        grid=(L // tm, N // tn),
        in_specs=[pl.BlockSpec((tm, D), lambda i, j: (i, 0)),
                  pl.BlockSpec((tn, D), lambda i, j: (j, 0)),
                  tab, tab, tab],
        out_specs=pl.BlockSpec((tn, tm), lambda i, j: (j, i)),
        scratch_shapes=[pltpu.VMEM((tm, D), BF16)],
        compiler_params=_cparams(("parallel", "arbitrary")),
        name="proj_q_t",
    )(x, wt, *tabs)


def _proj_v_t(x, wt, tk, tm=1024, tn=512):
    L, D = x.shape
    N = wt.shape[0]
    tm = min(tm, L)
    tk = min(tk, L)
    per = tm // tk
    return pl.pallas_call(
        functools.partial(_projT_kernel, rope=False, tn=tn, tk=tk),
        out_shape=jax.ShapeDtypeStruct((L // tk, N, tk), BF16),
        grid=(L // tm, N // tn),
        in_specs=[pl.BlockSpec((tm, D), lambda i, j: (i, 0)),
                  pl.BlockSpec((tn, D), lambda i, j: (j, 0))],
        out_specs=pl.BlockSpec((per, tn, tk), lambda i, j: (i, j, 0)),
        scratch_shapes=[pltpu.VMEM((tm, D), BF16)],
        compiler_params=_cparams(("parallel", "arbitrary")),
        name="proj_v_t",
    )(x, wt)


def _attn_kernel(qt_ref, k_ref, vt_ref, lq1_ref, lk1_ref, lq2_ref, lk2_ref, g_ref, o_ref,
                 acc_ref, l_ref, kn_ref, sa_ref, sb_ref, pa_ref, pb_ref, *, tk, nk):
    qt = qt_ref[...].astype(F32)
    tq = qt.shape[1]
    row = lax.broadcasted_iota(jnp.int32, qt.shape, 0)
    qs = (jnp.where(row < HEAD_DIM, qt, 0.0).astype(BF16),
          jnp.where(row >= HEAD_DIM, qt, 0.0).astype(BF16))
    zero = jnp.zeros((1, tq), F32)

    def k_chunk(j):
        return k_ref[pl.ds(pl.multiple_of(j * tk, tk), tk), :]

    @pl.when(pl.program_id(1) == 0)
    def _():
        def kbody(c, carry):
            kb = k_chunk(c).astype(F32)
            sq = kb * kb
            lane = lax.broadcasted_iota(jnp.int32, sq.shape, 1)
            n0 = jnp.sum(jnp.where(lane < HEAD_DIM, sq, 0.0), axis=1, keepdims=True)
            n1 = jnp.sum(jnp.where(lane >= HEAD_DIM, sq, 0.0), axis=1, keepdims=True)
            return (jnp.maximum(carry[0], jnp.max(n0, axis=0, keepdims=True)),
                    jnp.maximum(carry[1], jnp.max(n1, axis=0, keepdims=True)))

        z = jnp.zeros((1, 1), F32)
        k0, k1 = lax.fori_loop(0, nk, kbody, (z, z))
        kn_ref[0] = jnp.broadcast_to(k0, (1, LANES))
        kn_ref[1] = jnp.broadcast_to(k1, (1, LANES))

    q2 = qt * qt
    qn = (jnp.sum(q2[:HEAD_DIM], axis=0, keepdims=True), jnp.sum(q2[HEAD_DIM:], axis=0, keepdims=True))
    mb = tuple(jnp.sqrt(qn[mi] * kn_ref[mi][:, 0:1]) for mi in range(2))
    acc_ref[...] = jnp.zeros_like(acc_ref)

    def fixed_probs(j, p_ref):
        kb = k_chunk(j)
        sums = []
        for mi in range(2):
            pt = jnp.exp2(jnp.dot(kb, qs[mi], preferred_element_type=F32) - mb[mi])
            p_ref[mi] = pt.astype(BF16)
            sums.append(jnp.sum(pt, axis=0, keepdims=True))
        return tuple(sums)

    def fixed_accumulate(j, p_ref):
        vb = vt_ref[j]
        for mi in range(2):
            acc_ref[mi] += jnp.dot(vb, p_ref[mi], preferred_element_type=F32)

    def fixed_body(jj, carry):
        ls, la = carry
        j = 2 * jj
        lb = fixed_probs(j + 1, pb_ref)
        fixed_accumulate(j, pa_ref)
        ls = (ls[0] + la[0] + lb[0], ls[1] + la[1] + lb[1])
        la = fixed_probs(j + 2, pa_ref)
        fixed_accumulate(j + 1, pb_ref)
        return ls, la

    ls, la = lax.fori_loop(0, nk // 2 - 1, fixed_body, ((zero, zero), fixed_probs(0, pa_ref)))
    lb = fixed_probs(nk - 1, pb_ref)
    fixed_accumulate(nk - 2, pa_ref)
    fixed_accumulate(nk - 1, pb_ref)
    ls = (ls[0] + la[0] + lb[0], ls[1] + la[1] + lb[1])
    l_ref[0] = ls[0]
    l_ref[1] = ls[1]
    trusted = jnp.min(jnp.minimum(ls[0], ls[1])) >= FIXED_SHIFT_MIN_DENOM

    @pl.when(jnp.logical_not(trusted))
    def _():
        acc_ref[...] = jnp.zeros_like(acc_ref)
        neg = jnp.full((1, tq), -jnp.inf, F32)

        def scores(j, s_ref):
            kb = k_chunk(j)
            mx = []
            for mi in range(2):
                st = jnp.dot(kb, qs[mi], preferred_element_type=F32)
                s_ref[mi] = st
                mx.append(jnp.max(st, axis=0, keepdims=True))
            return tuple(mx)

        def consume(j, s_ref, mx, ms, lr):
            vb = vt_ref[j]
            new_m, new_l = [], []
            for mi in range(2):
                m_next = jnp.maximum(ms[mi], mx[mi])
                alpha = jnp.exp2(ms[mi] - m_next)
                pt = jnp.exp2(s_ref[mi] - m_next)
                new_l.append(alpha * lr[mi] + jnp.sum(pt, axis=0, keepdims=True))
                new_m.append(m_next)
                acc_ref[mi] = acc_ref[mi] * alpha + jnp.dot(vb, pt.astype(BF16), preferred_element_type=F32)
            return tuple(new_m), tuple(new_l)

        def body(jj, carry):
            ms, lr, mxa = carry
            j = 2 * jj
            mxb = scores(j + 1, sb_ref)
            ms, lr = consume(j, sa_ref, mxa, ms, lr)
            mxa = scores(jnp.minimum(j + 2, nk - 1), sa_ref)
            ms, lr = consume(j + 1, sb_ref, mxb, ms, lr)
            return ms, lr, mxa

        _, lr, _ = lax.fori_loop(0, nk // 2, body, ((neg, neg), (zero, zero), scores(0, sa_ref)))
        l_ref[0] = lr[0]
        l_ref[1] = lr[1]

    lam = (jnp.exp(jnp.sum(lq1_ref[...] * lk1_ref[...], axis=-1, keepdims=True))
           - jnp.exp(jnp.sum(lq2_ref[...] * lk2_ref[...], axis=-1, keepdims=True)) + LAMBDA_INIT)
    ot = acc_ref[0] / l_ref[0] - lam * (acc_ref[1] / l_ref[1])
    ot = ot * lax.rsqrt(jnp.mean(ot * ot, axis=0, keepdims=True) + LN_EPS)
    o_ref[...] = (ot.T * (g_ref[...] * (1.0 - LAMBDA_INIT))).astype(o_ref.dtype)


def _diff_attention(qt, k, vt, lq1, lk1, lq2, lk2, subln_g, tq=ATTN_TQ):
    L = k.shape[0]
    nk, _, tk = vt.shape
    assert nk % 2 == 0, "the kv loop handles chunks in pairs"
    tq = min(tq, L)
    vec = pl.BlockSpec((1, HEAD_DIM), lambda h, i: (0, 0))
    return pl.pallas_call(
        functools.partial(_attn_kernel, tk=tk, nk=nk),
        out_shape=jax.ShapeDtypeStruct((L, N_HEADS * VAL_DIM), BF16),
        grid=(N_HEADS, L // tq),
        in_specs=[pl.BlockSpec((LANES, tq), lambda h, i: (h, i)),
                  pl.BlockSpec((L, LANES), lambda h, i: (0, h)),
                  pl.BlockSpec((nk, VAL_DIM, tk), lambda h, i: (0, h, 0)),
                  vec, vec, vec, vec,
                  pl.BlockSpec((1, VAL_DIM), lambda h, i: (0, 0))],
        out_specs=pl.BlockSpec((tq, VAL_DIM), lambda h, i: (i, h)),
        scratch_shapes=[pltpu.VMEM((2, VAL_DIM, tq), F32), pltpu.VMEM((2, 1, tq), F32),
                        pltpu.VMEM((2, 1, LANES), F32),
                        pltpu.VMEM((2, tk, tq), F32), pltpu.VMEM((2, tk, tq), F32),
                        pltpu.VMEM((2, tk, tq), BF16), pltpu.VMEM((2, tk, tq), BF16)],
        compiler_params=_cparams(("parallel", "arbitrary")),
        name="diff_attention",
    )(qt, k, vt, lq1, lk1, lq2, lk2, subln_g)


def _conv_kernel(x_ref, w_ref, b_ref, o_ref, *, rows, nchunks):
    w0 = w_ref[0:1, :]
    w1 = w_ref[1:2, :]
    w2 = w_ref[2:3, :]
    b = b_ref[...]
    grp = BF16_SUBLANES
    last = nchunks * rows - grp

    def body(i, carry):
        r0 = pl.multiple_of(i * rows, rows)
        x0 = x_ref[pl.ds(r0, rows), :].astype(F32)
        pr = x_ref[pl.ds(pl.multiple_of(jnp.maximum(r0 - grp, 0), grp), grp), :].astype(F32)
        nx = x_ref[pl.ds(pl.multiple_of(jnp.minimum(r0 + rows, last), grp), grp), :].astype(F32)
        prev_row = jnp.where(i > 0, pr[grp - 1:grp, :], 0.0)
        next_row = jnp.where(i < nchunks - 1, nx[0:1, :], 0.0)
        row = lax.broadcasted_iota(jnp.int32, x0.shape, 0)
        xm = jnp.where(row == 0, prev_row, pltpu.roll(x0, 1, 0))
        xp = jnp.where(row == rows - 1, next_row, pltpu.roll(x0, rows - 1, 0))
        o_ref[pl.ds(r0, rows), :] = (xm * w0 + x0 * w1 + xp * w2 + b).astype(o_ref.dtype)
        return carry

    lax.fori_loop(0, nchunks, body, 0)


def _short_conv(p, w, b, tn=256, rows=512):
    L, C = p.shape
    rows = min(rows, L)
    return pl.pallas_call(
        functools.partial(_conv_kernel, rows=rows, nchunks=L // rows),
        out_shape=jax.ShapeDtypeStruct((L, C), BF16),
        grid=(C // tn,),
        in_specs=[pl.BlockSpec((L, tn), lambda j: (0, j)),
                  pl.BlockSpec((3, tn), lambda j: (0, j)),
                  pl.BlockSpec((1, tn), lambda j: (0, j))],
        out_specs=pl.BlockSpec((L, tn), lambda j: (0, j)),
        compiler_params=_cparams(("parallel",)),
        name="short_conv",
    )(p, w, b)


def _split_bf16(a):
    hi = a.astype(BF16)
    return hi, (a - hi.astype(F32)).astype(BF16)


def _filter_kernel(z_ref, w1_ref, b1_ref, w2_ref, b2_ref, w3_ref, b3_ref, fr_ref, wfh_ref, wfl_ref,
                   wbh_ref, wbl_ref, dl_ref, bias_ref, o_ref, *, L, nh):
    hp = lax.Precision.HIGHEST
    fr = fr_ref[...]
    h = jnp.sin(fr * (jnp.dot(z_ref[...], w1_ref[...], precision=hp, preferred_element_type=F32) + b1_ref[...]))
    h = jnp.sin(fr * (jnp.dot(h, w2_ref[...], precision=hp, preferred_element_type=F32) + b2_ref[...]))
    h = jnp.sin(fr * (jnp.dot(h, w3_ref[...], precision=hp, preferred_element_type=F32) + b3_ref[...]))
    hh, hl = _split_bf16(h)

    def out_proj(wh_ref, wl_ref):
        wh = wh_ref[...]
        return (jnp.dot(hh, wh, preferred_element_type=F32) + jnp.dot(hl, wh, preferred_element_type=F32)
                + jnp.dot(hh, wl_ref[...], preferred_element_type=F32))

    rb = z_ref.shape[0]
    nq = rb // nh
    c2 = dl_ref.shape[1]
    row = lax.broadcasted_iota(jnp.int32, (rb, 1), 0)
    assert nh & (nh - 1) == 0
    r = jnp.bitwise_and(row, nh - 1)
    q_idx = lax.shift_right_logical(row, nh.bit_length() - 1)
    n = FFT_MINOR * r + nq * pl.program_id(0) + q_idx
    step = 1.0 / (L - 1)
    dl = dl_ref[...]
    kf = out_proj(wfh_ref, wfl_ref) * jnp.exp(-(n.astype(F32) * step) * dl)
    kf = (kf + jnp.where(n == 0, bias_ref[...], 0.0)).astype(o_ref.dtype)
    kb = out_proj(wbh_ref, wbl_ref) * jnp.exp(-((L - n).astype(F32) * step) * dl)
    kb = jnp.where(n == 0, 0.0, kb).astype(o_ref.dtype)
    for q in range(nq):
        o_ref[0:nh, q * c2:(q + 1) * c2] = kf[q * nh:(q + 1) * nh]
        o_ref[nh:2 * nh, q * c2:(q + 1) * c2] = kb[q * nh:(q + 1) * nh]


def _filter_features(L, nq):
    half = LANES // 2
    nh = L // FFT_MINOR
    bands = (FILTER_EMB - 1) // 2
    f = np.linspace(1e-4, bands - 1, bands)
    z = np.zeros((L, LANES))
    step, q, r = np.meshgrid(np.arange(FFT_MINOR // nq), np.arange(nq), np.arange(nh), indexing="ij")
    n = (FFT_MINOR * r + nq * step + q).reshape(-1).astype(np.float64)
    for off, pos in ((0, n), (half, L - n)):
        w = 2.0 * np.pi * pos / L
        z[:, off] = pos / (L - 1)
        z[:, off + 1:off + 1 + bands] = np.cos(f[None, :] * w[:, None])
        z[:, off + 1 + bands:off + 1 + 2 * bands] = -np.sin(f[None, :] * w[:, None])
    return jnp.asarray(z, F32)


def _pad_to(a, shape):
    return jnp.pad(a, [(0, s - d) for d, s in zip(a.shape, shape)])


def _block_diag2(a):
    half = LANES // 2
    p = _pad_to(a, (half, half))
    z = jnp.zeros_like(p)
    return jnp.concatenate([jnp.concatenate([p, z], axis=1), jnp.concatenate([z, p], axis=1)], axis=0)


def _hyena_filters(L, fw1, fb1, fw2, fb2, fw3, fb3, ffreq, fw_out, bias, nq=8):
    H = LANES
    half = H // 2
    C2 = 2 * D_HYENA
    nh = L // FFT_MINOR
    rb = nq * nh
    dup = lambda v: jnp.tile(_pad_to(v[None, :], (1, half)), (1, 2))
    wo = fw_out.reshape(FILTER_HIDDEN, 2, 2, D_HYENA)
    wf = _pad_to(wo[:, :, 0, :].reshape(FILTER_HIDDEN, C2), (H, C2))
    wb = jnp.concatenate([jnp.zeros((half, C2), F32), _pad_to(wo[:, :, 1, :].reshape(FILTER_HIDDEN, C2), (half, C2))])
    deltas = np.abs(np.linspace(MIN_DECAY, MAX_DECAY, D_HYENA))
    dl = jnp.asarray(np.tile(deltas, 2)[None, :], F32)
    full = lambda shape: pl.BlockSpec(shape, lambda i: (0,) * len(shape))
    return pl.pallas_call(
        functools.partial(_filter_kernel, L=L, nh=nh),
        out_shape=jax.ShapeDtypeStruct((2 * nh, FFT_MINOR * C2), BF16),
        grid=(L // rb,),
        in_specs=[pl.BlockSpec((rb, H), lambda i: (i, 0)),
                  full((H, H)), full((1, H)), full((H, H)), full((1, H)), full((H, H)), full((1, H)),
                  full((1, H)),
                  full((H, C2)), full((H, C2)), full((H, C2)), full((H, C2)),
                  full((1, C2)), full((1, C2))],
        out_specs=pl.BlockSpec((2 * nh, nq * C2), lambda i: (0, i)),
        compiler_params=_cparams(("parallel",)),
        name="hyena_filters",
    )(_filter_features(L, nq), _block_diag2(fw1), dup(fb1), _block_diag2(fw2), dup(fb2), _block_diag2(fw3),
      dup(fb3), dup(ffreq), *_split_bf16(wf), *_split_bf16(wb), dl, bias.reshape(1, C2))


def _dft_constants(L):
    N = 2 * L
    N1 = N // FFT_MINOR
    nh = N1 // 2
    k1h = N1 // 2 + 1
    k1p = -(-k1h // 8) * 8
    k1 = np.arange(k1h)[:, None]

    def stage_a(nn):
        th = 2 * np.pi * np.arange(nn)[None, :] * k1 / N1
        fa = np.zeros((2 * k1p, nn))
        fa[:k1h] = np.cos(th)
        fa[k1p:k1p + k1h] = -np.sin(th)
        return fa

    n2 = np.arange(FFT_MINOR)
    m1 = np.zeros((k1p, 2 * FFT_MINOR, 2 * FFT_MINOR))
    m2 = np.zeros_like(m1)
    for a in range(k1h):
        ph = -2 * np.pi * (n2[None, :] * a / N + n2[None, :] * n2[:, None] / FFT_MINOR)
        cr, ci = np.cos(ph), np.sin(ph)
        m1[a] = np.block([[cr, -ci], [ci, cr]])
        m2[a] = np.block([[cr.T, ci.T], [-ci.T, cr.T]])
    kk = np.arange(k1h)[None, :]
    cw = np.where((kk == 0) | (kk == N1 // 2), 1.0, 2.0) / N
    th = 2 * np.pi * np.arange(nh)[:, None] * kk / N1
    g = np.zeros((nh, 2 * k1p))
    g[:, :k1h] = cw * np.cos(th)
    g[:, k1p:k1p + k1h] = -cw * np.sin(th)
    as_bf16 = lambda a: jnp.asarray(a, F32).astype(BF16)
    return dict(nh=nh, n1=N1, k1h=k1h, k1p=k1p, fa_data=as_bf16(stage_a(nh)), fa_filt=as_bf16(stage_a(N1)),
                m1=as_bf16(m1), m2=as_bf16(m2), g=as_bf16(g))


def _lmat_kernel(f_ref, x_ref, o_ref):
    o_ref[...] = jnp.dot(f_ref[...], x_ref[...].astype(BF16), preferred_element_type=F32).astype(o_ref.dtype)


def _lmat(fmat, x2d, out_dtype=BF16, tn=8192):
    M, K = fmat.shape
    ncols = x2d.shape[1]
    tn = min(tn, ncols)
    return pl.pallas_call(
        _lmat_kernel,
        out_shape=jax.ShapeDtypeStruct((M, ncols), out_dtype),
        grid=(ncols // tn,),
        in_specs=[pl.BlockSpec((M, K), lambda j: (0, 0)),
                  pl.BlockSpec((K, tn), lambda j: (0, j))],
        out_specs=pl.BlockSpec((M, tn), lambda j: (0, j)),
        compiler_params=_cparams(("parallel",)),
        name="dft_stage_a",
    )(fmat, x2d)


def _lmat_gate_kernel(g_ref, z_ref, gate_ref, o_ref):
    o_ref[...] = jnp.dot(g_ref[...], z_ref[...], preferred_element_type=F32) * gate_ref[...]


def _lmat_gate(gmat, z2d, gate2d, tn=8192):
    M, K = gmat.shape
    ncols = z2d.shape[1]
    tn = min(tn, ncols)
    return pl.pallas_call(
        _lmat_gate_kernel,
        out_shape=jax.ShapeDtypeStruct((M, ncols), F32),
        grid=(ncols // tn,),
        in_specs=[pl.BlockSpec((M, K), lambda j: (0, 0)),
                  pl.BlockSpec((K, tn), lambda j: (0, j)),
                  pl.BlockSpec((M, tn), lambda j: (0, j))],
        out_specs=pl.BlockSpec((M, tn), lambda j: (0, j)),
        compiler_params=_cparams(("parallel",)),
        name="dft_stage_a_inv",
    )(gmat, z2d, gate2d)


K1_PER_STEP = 4


def _filter_spectrum_kernel(m1_ref, a_ref, o_ref):
    for b in range(m1_ref.shape[0]):
        x = jnp.concatenate([a_ref[0, b], a_ref[1, b]], axis=0)
        o_ref[b] = jnp.dot(m1_ref[b], x, preferred_element_type=F32).astype(o_ref.dtype)


def _filter_spectrum(m1, a4):
    _, k1p, n2, C = a4.shape
    kb = K1_PER_STEP
    return pl.pallas_call(
        _filter_spectrum_kernel,
        out_shape=jax.ShapeDtypeStruct((k1p, 2 * n2, C), BF16),
        grid=(k1p // kb,),
        in_specs=[pl.BlockSpec((kb, 2 * n2, 2 * n2), lambda a: (a, 0, 0)),
                  pl.BlockSpec((2, kb, n2, C), lambda a: (0, a, 0, 0))],
        out_specs=pl.BlockSpec((kb, 2 * n2, C), lambda a: (a, 0, 0)),
        compiler_params=_cparams(("parallel",)),
        name="filter_spectrum",
    )(m1, a4)


def _middle_kernel(m1_ref, m2_ref, a_ref, kf_ref, o_ref):
    n2 = a_ref.shape[2]
    for b in range(m1_ref.shape[0]):
        x = jnp.concatenate([a_ref[0, b], a_ref[1, b]], axis=0)
        s = jnp.dot(m1_ref[b], x, preferred_element_type=F32)
        sr, si = s[:n2], s[n2:]
        kf = kf_ref[b].astype(F32)
        kr, ki = kf[:n2], kf[n2:]
        y = jnp.concatenate([sr * kr - si * ki, sr * ki + si * kr], axis=0).astype(BF16)
        z = jnp.dot(m2_ref[b], y, preferred_element_type=F32).astype(o_ref.dtype)
        o_ref[0, b] = z[:n2]
        o_ref[1, b] = z[n2:]


def _middle(m1, m2, a4, kf, order):
    _, k1p, n2, C = a4.shape
    kb = K1_PER_STEP
    mat = pl.BlockSpec((kb, 2 * n2, 2 * n2), lambda a: (a, 0, 0))
    return pl.pallas_call(
        _middle_kernel,
        out_shape=jax.ShapeDtypeStruct((2, k1p, n2, C), BF16),
        grid=(k1p // kb,),
        in_specs=[mat, mat,
                  pl.BlockSpec((2, kb, n2, C), lambda a: (0, a, 0, 0)),
                  pl.BlockSpec((kb, 2 * n2, C), lambda a: (a, 0, order))],
        out_specs=pl.BlockSpec((2, kb, n2, C), lambda a: (0, a, 0, 0)),
        compiler_params=_cparams(("parallel",)),
        name="dft_middle",
    )(m1, m2, a4, kf)


def _hyena_long_convs(v, g1, kfilt2d, L):
    C = v.shape[1]
    cs = _dft_constants(L)
    nh, k1p = cs["nh"], cs["k1p"]
    fa = _lmat(cs["fa_filt"], kfilt2d)
    kspec = _filter_spectrum(cs["m1"], fa.reshape(2, k1p, FFT_MINOR, 2 * C))
    a = _lmat(cs["fa_data"], v.reshape(nh, FFT_MINOR * C))
    z = _middle(cs["m1"], cs["m2"], a.reshape(2, k1p, FFT_MINOR, C), kspec, 0)
    y2d = _lmat_gate(cs["g"], z.reshape(2 * k1p, FFT_MINOR * C), g1.reshape(nh, FFT_MINOR * C))
    a = _lmat(cs["fa_data"], y2d)
    z = _middle(cs["m1"], cs["m2"], a.reshape(2, k1p, FFT_MINOR, C), kspec, 1)
    return _lmat(cs["g"], z.reshape(2 * k1p, FFT_MINOR * C), F32).reshape(L, C)


def _out_kernel(yh_ref, gate_ref, ya_ref, wh_ref, wa_ref, ng_ref, x_ref, g_ref, b_ref, o_ref):
    yh = yh_ref[...] * gate_ref[...]
    yh = yh * lax.rsqrt(jnp.mean(yh * yh, axis=-1, keepdims=True) + LN_EPS) * ng_ref[...]
    acc = jnp.dot(yh.astype(BF16), wh_ref[...], preferred_element_type=F32)
    acc = acc + jnp.dot(ya_ref[...], wa_ref[...], preferred_element_type=F32)
    o_ref[...] = _layer_norm(ALPHA * x_ref[...] + acc, g_ref[...], b_ref[...])


def _out_ln(y_hy, z, gate_block, y_at, w_out, norm_g, x1, g, b, tm=512):
    L, D = x1.shape
    ch, ca = y_hy.shape[1], y_at.shape[1]
    tm = min(tm, L)
    row = lambda c: pl.BlockSpec((tm, c), lambda i: (i, 0))
    full = lambda r, c: pl.BlockSpec((r, c), lambda i: (0, 0))
    return pl.pallas_call(
        _out_kernel,
        out_shape=jax.ShapeDtypeStruct((L, D), F32),
        grid=(L // tm,),
        in_specs=[row(ch), pl.BlockSpec((tm, ch), lambda i: (i, gate_block)), row(ca),
                  pl.BlockSpec((ch, D), lambda i: (0, 0)),
                  pl.BlockSpec((ca, D), lambda i: (1, 0)),
                  full(1, ch), row(D), full(1, D), full(1, D)],
        out_specs=row(D),
        compiler_params=_cparams(("parallel",)),
        name="out_ln",
    )(y_hy, z, y_at, w_out, w_out, norm_g, x1, g, b)


def kernel(x, ffn1_w_gate, ffn1_w_up, ffn1_w_down, ln1_g, ln1_b, w_in, hyena_conv_w, hyena_conv_b, filt_w1, filt_b1, filt_w2, filt_b2, filt_w3, filt_b3, filt_freq, filt_w_out, hyena_bias, hyena_norm_g, lambda_q1, lambda_k1, lambda_q2, lambda_k2, subln_g, w_out, ln2_g, ln2_b, ffn2_w_gate, ffn2_w_up, ffn2_w_down, ln3_g, ln3_b):
    assert x.shape[0] == 1 and ffn1_w_gate.shape[0] == DEPTH
    L = x.shape[1]
    bf = lambda a: a.astype(BF16)
    row = lambda a: a.reshape(1, -1)
    h = x[0]
    s_hy = 3 * D_HYENA
    for i in range(DEPTH):
        h = _ffn_ln(h, bf(ffn1_w_gate[i]), bf(ffn1_w_up[i]), bf(ffn1_w_down[i]), row(ln1_g[i]), row(ln1_b[i]))
        w_i = w_in[i]
        p_hy = _proj(h, bf(w_i[:, :s_hy]), BF16)
        q_t = _proj_q_t(h, bf(w_i[:, s_hy:s_hy + D_QK].T))
        k_at = _proj_rope(h, bf(w_i[:, s_hy + D_QK:s_hy + 2 * D_QK]))
        v_t = _proj_v_t(h, bf(w_i[:, s_hy + 2 * D_QK:].T), ATTN_TK)
        z = _short_conv(p_hy, hyena_conv_w[i], row(hyena_conv_b[i]))
        kfilt = _hyena_filters(L, filt_w1[i], filt_b1[i], filt_w2[i], filt_b2[i], filt_w3[i], filt_b3[i],
                               filt_freq[i], filt_w_out[i], hyena_bias[i])
        y_hy = _hyena_long_convs(z[:, :D_HYENA], z[:, D_HYENA:2 * D_HYENA], kfilt, L)
        y_at = _diff_attention(q_t, k_at, v_t, row(lambda_q1[i]), row(lambda_k1[i]), row(lambda_q2[i]),
                               row(lambda_k2[i]), row(subln_g[i]))
        h = _out_ln(y_hy, z, 2, y_at, bf(w_out[i]), row(hyena_norm_g[i]), h, row(ln2_g[i]), row(ln2_b[i]))
        h = _ffn_ln(h, bf(ffn2_w_gate[i]), bf(ffn2_w_up[i]), bf(ffn2_w_down[i]), row(ln3_g[i]), row(ln3_b[i]))
    return h[None]
```

```python
import functools
import math

import numpy as np
import jax
import jax.numpy as jnp
from jax import lax
from jax.experimental import pallas as pl
from jax.experimental.pallas import tpu as pltpu

F32 = jnp.float32
BF16 = jnp.bfloat16

D_HYENA = 1024
N_HEADS = 8
HEAD_DIM = 64
VAL_DIM = 128
D_QK = 2 * N_HEADS * HEAD_DIM
ROT_DIM = 16
ROPE_THETA = 500000.0
FILTER_EMB = 33
FILTER_HIDDEN = 64
DECAY_TARGET = 1e-2
MIN_DECAY = math.log(DECAY_TARGET) / 1.5
MAX_DECAY = math.log(DECAY_TARGET) / 0.3
LN_EPS = 1e-5
DEPTH = 1
ALPHA = (2.0 * DEPTH) ** 0.25
LAMBDA_INIT = 0.8 - 0.6 * math.exp(-0.3 * 0)
LOG2E = math.log2(math.e)

LANES = 128
BF16_SUBLANES = 16
FFT_MINOR = 128
VMEM_LIMIT = 56 * 1024 * 1024
ATTN_TQ = 512
ATTN_TK = 1024
FIXED_SHIFT_MIN_DENOM = 2.0 ** -60


def _cparams(sem):
    return pltpu.CompilerParams(dimension_semantics=sem, vmem_limit_bytes=VMEM_LIMIT)


def _layer_norm(y, g, b):
    mu = jnp.mean(y, axis=-1, keepdims=True)
    d = y - mu
    var = jnp.mean(d * d, axis=-1, keepdims=True)
    return d * lax.rsqrt(var + LN_EPS) * g + b


def _ffn_kernel(x_ref, wg_ref, wu_ref, wd_ref, g_ref, b_ref, o_ref, xb_ref, acc_ref, h_ref, *, nf):
    f = pl.program_id(1)

    def up(slot):
        xb = xb_ref[...]
        hg = jnp.dot(xb, wg_ref[...], preferred_element_type=F32)
        hu = jnp.dot(xb, wu_ref[...], preferred_element_type=F32)
        h_ref[slot] = (hg * jax.nn.sigmoid(hg) * hu).astype(BF16)

    def down(slot):
        acc_ref[...] += jnp.dot(h_ref[slot], wd_ref[...], preferred_element_type=F32)

    @pl.when(f == 0)
    def _():
        xb_ref[...] = x_ref[...].astype(BF16)
        acc_ref[...] = jnp.zeros_like(acc_ref)
        up(0)

    @pl.when(jnp.logical_and(f > 0, f < nf))
    def _():
        down((f - 1) % 2)
        up(f % 2)

    @pl.when(f == nf)
    def _():
        down((nf - 1) % 2)
        y = ALPHA * x_ref[...] + 0.5 * acc_ref[...]
        o_ref[...] = _layer_norm(y, g_ref[...], b_ref[...])


def _ffn_ln(x, wg, wu, wd, g, b, tm=1024, tf=512):
    L, D = x.shape
    nf = wg.shape[1] // tf
    tm = min(tm, L)
    up_w = pl.BlockSpec((D, tf), lambda i, f: (0, jnp.minimum(f, nf - 1)))
    once = dict(pipeline_mode=pl.Buffered(1))
    return pl.pallas_call(
        functools.partial(_ffn_kernel, nf=nf),
        out_shape=jax.ShapeDtypeStruct((L, D), F32),
        grid=(L // tm, nf + 1),
        in_specs=[
            pl.BlockSpec((tm, D), lambda i, f: (i, 0), **once),
            up_w, up_w,
            pl.BlockSpec((tf, D), lambda i, f: (jnp.maximum(f - 1, 0), 0)),
            pl.BlockSpec((1, D), lambda i, f: (0, 0)),
            pl.BlockSpec((1, D), lambda i, f: (0, 0)),
        ],
        out_specs=pl.BlockSpec((tm, D), lambda i, f: (i, 0), **once),
        scratch_shapes=[pltpu.VMEM((tm, D), BF16), pltpu.VMEM((tm, D), F32), pltpu.VMEM((2, tm, tf), BF16)],
        compiler_params=_cparams(("parallel", "arbitrary")),
        name="ffn_ln",
    )(x, wg, wu, wd, g, b)


def _proj_kernel(x_ref, w_ref, o_ref, xb_ref):
    @pl.when(pl.program_id(1) == 0)
    def _():
        xb_ref[...] = x_ref[...].astype(BF16)

    o_ref[...] = jnp.dot(xb_ref[...], w_ref[...], preferred_element_type=F32).astype(o_ref.dtype)


def _proj(x, w, out_dtype, tm=1024, tn=1024):
    L, D = x.shape
    N = w.shape[1]
    tm = min(tm, L)
    return pl.pallas_call(
        _proj_kernel,
        out_shape=jax.ShapeDtypeStruct((L, N), out_dtype),
        grid=(L // tm, N // tn),
        in_specs=[pl.BlockSpec((tm, D), lambda i, j: (i, 0)),
                  pl.BlockSpec((D, tn), lambda i, j: (0, j))],
        out_specs=pl.BlockSpec((tm, tn), lambda i, j: (i, j)),
        scratch_shapes=[pltpu.VMEM((tm, D), BF16)],
        compiler_params=_cparams(("parallel", "arbitrary")),
        name="proj",
    )(x, w)


def _rope_tables(L, scale):
    half = ROT_DIM // 2
    inv = ROPE_THETA ** (-np.arange(0, ROT_DIM, 2, dtype=np.float64) / ROT_DIM)
    pos = np.arange(L, dtype=np.float64)[:, None]
    d = np.arange(LANES) % HEAD_DIM
    ang = pos * inv[d % half][None, :]
    c = np.where(d[None, :] < ROT_DIM, np.cos(ang), 1.0) * scale
    s1 = np.where(d[None, :] < half, -np.sin(ang), 0.0) * scale
    s2 = np.where((d[None, :] >= half) & (d[None, :] < ROT_DIM), np.sin(ang), 0.0) * scale
    return c, s1, s2


def _proj_rope_kernel(x_ref, w_ref, c_ref, s1_ref, s2_ref, o_ref, xb_ref, *, tn):
    @pl.when(pl.program_id(1) == 0)
    def _():
        xb_ref[...] = x_ref[...].astype(BF16)

    p = jnp.dot(xb_ref[...], w_ref[...], preferred_element_type=F32)
    c, s1, s2 = c_ref[...], s1_ref[...], s2_ref[...]
    shift = ROT_DIM // 2
    for h in range(tn // LANES):
        ph = p[:, h * LANES:(h + 1) * LANES]
        r = ph * c + pltpu.roll(ph, LANES - shift, 1) * s1 + pltpu.roll(ph, shift, 1) * s2
        o_ref[:, h * LANES:(h + 1) * LANES] = r.astype(o_ref.dtype)


def _proj_rope(x, w, tm=1024, tn=1024):
    L, D = x.shape
    N = w.shape[1]
    tm = min(tm, L)
    tabs = [jnp.asarray(t, F32) for t in _rope_tables(L, 1.0)]
    tab = pl.BlockSpec((tm, LANES), lambda i, j: (i, 0))
    return pl.pallas_call(
        functools.partial(_proj_rope_kernel, tn=tn),
        out_shape=jax.ShapeDtypeStruct((L, N), BF16),
        grid=(L // tm, N // tn),
        in_specs=[pl.BlockSpec((tm, D), lambda i, j: (i, 0)),
                  pl.BlockSpec((D, tn), lambda i, j: (0, j)),
                  tab, tab, tab],
        out_specs=pl.BlockSpec((tm, tn), lambda i, j: (i, j)),
        scratch_shapes=[pltpu.VMEM((tm, D), BF16)],
        compiler_params=_cparams(("parallel", "arbitrary")),
        name="proj_rope",
    )(x, w, *tabs)


def _projT_kernel(x_ref, wt_ref, *rest, rope, tn, tk):
    if rope:
        c_ref, s1_ref, s2_ref, o_ref, xb_ref = rest
    else:
        o_ref, xb_ref = rest

    @pl.when(pl.program_id(1) == 0)
    def _():
        xb_ref[...] = x_ref[...].astype(BF16)

    pt = lax.dot_general(wt_ref[...], xb_ref[...], (((1,), (1,)), ((), ())), preferred_element_type=F32)
    if rope:
        c, s1, s2 = c_ref[...], s1_ref[...], s2_ref[...]
        shift = ROT_DIM // 2
        for h in range(tn // LANES):
            ph = pt[h * LANES:(h + 1) * LANES, :]
            r = ph * c + pltpu.roll(ph, LANES - shift, 0) * s1 + pltpu.roll(ph, shift, 0) * s2
            o_ref[h * LANES:(h + 1) * LANES, :] = r.astype(o_ref.dtype)
    else:
        for s in range(pt.shape[1] // tk):
            o_ref[s] = pt[:, s * tk:(s + 1) * tk].astype(o_ref.dtype)


def _proj_q_t(x, wt, tm=1024, tn=512):
    L, D = x.shape
    N = wt.shape[0]
    tm = min(tm, L)
    tabs = [jnp.asarray(t.T, F32) for t in _rope_tables(L, HEAD_DIM ** -0.5 * LOG2E)]
    tab = pl.BlockSpec((LANES, tm), lambda i, j: (0, i))
    return pl.pallas_call(
        functools.partial(_projT_kernel, rope=True, tn=tn, tk=None),
        out_shape=jax.ShapeDtypeStruct((N, L), BF16),
        grid=(L // tm, N // tn),
        in_specs=[pl.BlockSpec((tm, D), lambda i, j: (i, 0)),
                  pl.BlockSpec((tn, D), lambda i, j: (j, 0)),
                  tab, tab, tab],
        out_specs=pl.BlockSpec((tn, tm), lambda i, j: (j, i)),
        scratch_shapes=[pltpu.VMEM((tm, D), BF16)],
        compiler_params=_cparams(("parallel", "arbitrary")),
        name="proj_q_t",
    )(x, wt, *tabs)


def _proj_v_t(x, wt, tk, tm=1024, tn=512):
    L, D = x.shape
    N = wt.shape[0]
    tm = min(tm, L)
    tk = min(tk, L)
    per = tm // tk
    return pl.pallas_call(
        functools.partial(_projT_kernel, rope=False, tn=tn, tk=tk),
        out_shape=jax.ShapeDtypeStruct((L // tk, N, tk), BF16),
        grid=(L // tm, N // tn),
        in_specs=[pl.BlockSpec((tm, D), lambda i, j: (i, 0)),
                  pl.BlockSpec((tn, D), lambda i, j: (j, 0))],
        out_specs=pl.BlockSpec((per, tn, tk), lambda i, j: (i, j, 0)),
        scratch_shapes=[pltpu.VMEM((tm, D), BF16)],
        compiler_params=_cparams(("parallel", "arbitrary")),
        name="proj_v_t",
    )(x, wt)


def _attn_kernel(qt_ref, k_ref, vt_ref, lq1_ref, lk1_ref, lq2_ref, lk2_ref, g_ref, o_ref,
                 acc_ref, l_ref, kn_ref, sa_ref, sb_ref, pa_ref, pb_ref, *, tk, nk):
    qt = qt_ref[...].astype(F32)
    tq = qt.shape[1]
    row = lax.broadcasted_iota(jnp.int32, qt.shape, 0)
    qs = (jnp.where(row < HEAD_DIM, qt, 0.0).astype(BF16),
          jnp.where(row >= HEAD_DIM, qt, 0.0).astype(BF16))
    zero = jnp.zeros((1, tq), F32)

    def k_chunk(j):
        return k_ref[pl.ds(pl.multiple_of(j * tk, tk), tk), :]

    @pl.when(pl.program_id(1) == 0)
    def _():
        def kbody(c, carry):
            kb = k_chunk(c).astype(F32)
            sq = kb * kb
            lane = lax.broadcasted_iota(jnp.int32, sq.shape, 1)
            n0 = jnp.sum(jnp.where(lane < HEAD_DIM, sq, 0.0), axis=1, keepdims=True)
            n1 = jnp.sum(jnp.where(lane >= HEAD_DIM, sq, 0.0), axis=1, keepdims=True)
            return (jnp.maximum(carry[0], jnp.max(n0, axis=0, keepdims=True)),
                    jnp.maximum(carry[1], jnp.max(n1, axis=0, keepdims=True)))

        z = jnp.zeros((1, 1), F32)
        k0, k1 = lax.fori_loop(0, nk, kbody, (z, z))
        kn_ref[0] = jnp.broadcast_to(k0, (1, LANES))
        kn_ref[1] = jnp.broadcast_to(k1, (1, LANES))

    q2 = qt * qt
    qn = (jnp.sum(q2[:HEAD_DIM], axis=0, keepdims=True), jnp.sum(q2[HEAD_DIM:], axis=0, keepdims=True))
    mb = tuple(jnp.sqrt(qn[mi] * kn_ref[mi][:, 0:1]) for mi in range(2))
    acc_ref[...] = jnp.zeros_like(acc_ref)

    def fixed_probs(j, p_ref):
        kb = k_chunk(j)
        sums = []
        for mi in range(2):
            pt = jnp.exp2(jnp.dot(kb, qs[mi], preferred_element_type=F32) - mb[mi])
            p_ref[mi] = pt.astype(BF16)
            sums.append(jnp.sum(pt, axis=0, keepdims=True))
        return tuple(sums)

    def fixed_accumulate(j, p_ref):
        vb = vt_ref[j]
        for mi in range(2):
            acc_ref[mi] += jnp.dot(vb, p_ref[mi], preferred_element_type=F32)

    def fixed_body(jj, carry):
        ls, la = carry
        j = 2 * jj
        lb = fixed_probs(j + 1, pb_ref)
        fixed_accumulate(j, pa_ref)
        ls = (ls[0] + la[0] + lb[0], ls[1] + la[1] + lb[1])
        la = fixed_probs(j + 2, pa_ref)
        fixed_accumulate(j + 1, pb_ref)
        return ls, la

    ls, la = lax.fori_loop(0, nk // 2 - 1, fixed_body, ((zero, zero), fixed_probs(0, pa_ref)))
    lb = fixed_probs(nk - 1, pb_ref)
    fixed_accumulate(nk - 2, pa_ref)
    fixed_accumulate(nk - 1, pb_ref)
    ls = (ls[0] + la[0] + lb[0], ls[1] + la[1] + lb[1])
    l_ref[0] = ls[0]
    l_ref[1] = ls[1]
    trusted = jnp.min(jnp.minimum(ls[0], ls[1])) >= FIXED_SHIFT_MIN_DENOM

    @pl.when(jnp.logical_not(trusted))
    def _():
        acc_ref[...] = jnp.zeros_like(acc_ref)
        neg = jnp.full((1, tq), -jnp.inf, F32)

        def scores(j, s_ref):
            kb = k_chunk(j)
            mx = []
            for mi in range(2):
                st = jnp.dot(kb, qs[mi], preferred_element_type=F32)
                s_ref[mi] = st
                mx.append(jnp.max(st, axis=0, keepdims=True))
            return tuple(mx)

        def consume(j, s_ref, mx, ms, lr):
            vb = vt_ref[j]
            new_m, new_l = [], []
            for mi in range(2):
                m_next = jnp.maximum(ms[mi], mx[mi])
                alpha = jnp.exp2(ms[mi] - m_next)
                pt = jnp.exp2(s_ref[mi] - m_next)
                new_l.append(alpha * lr[mi] + jnp.sum(pt, axis=0, keepdims=True))
                new_m.append(m_next)
                acc_ref[mi] = acc_ref[mi] * alpha + jnp.dot(vb, pt.astype(BF16), preferred_element_type=F32)
            return tuple(new_m), tuple(new_l)

        def body(jj, carry):
            ms, lr, mxa = carry
            j = 2 * jj
            mxb = scores(j + 1, sb_ref)
            ms, lr = consume(j, sa_ref, mxa, ms, lr)
            mxa = scores(jnp.minimum(j + 2, nk - 1), sa_ref)
            ms, lr = consume(j + 1, sb_ref, mxb, ms, lr)
            return ms, lr, mxa

        _, lr, _ = lax.fori_loop(0, nk // 2, body, ((neg, neg), (zero, zero), scores(0, sa_ref)))
        l_ref[0] = lr[0]
        l_ref[1] = lr[1]

    lam = (jnp.exp(jnp.sum(lq1_ref[...] * lk1_ref[...], axis=-1, keepdims=True))
           - jnp.exp(jnp.sum(lq2_ref[...] * lk2_ref[...], axis=-1, keepdims=True)) + LAMBDA_INIT)
    ot = acc_ref[0] / l_ref[0] - lam * (acc_ref[1] / l_ref[1])
    ot = ot * lax.rsqrt(jnp.mean(ot * ot, axis=0, keepdims=True) + LN_EPS)
    o_ref[...] = (ot.T * (g_ref[...] * (1.0 - LAMBDA_INIT))).astype(o_ref.dtype)


def _diff_attention(qt, k, vt, lq1, lk1, lq2, lk2, subln_g, tq=ATTN_TQ):
    L = k.shape[0]
    nk, _, tk = vt.shape
    assert nk % 2 == 0, "the kv loop handles chunks in pairs"
    tq = min(tq, L)
    vec = pl.BlockSpec((1, HEAD_DIM), lambda h, i: (0, 0))
    return pl.pallas_call(
        functools.partial(_attn_kernel, tk=tk, nk=nk),
        out_shape=jax.ShapeDtypeStruct((L, N_HEADS * VAL_DIM), BF16),
        grid=(N_HEADS, L // tq),
        in_specs=[pl.BlockSpec((LANES, tq), lambda h, i: (h, i)),
                  pl.BlockSpec((L, LANES), lambda h, i: (0, h)),
                  pl.BlockSpec((nk, VAL_DIM, tk), lambda h, i: (0, h, 0)),
                  vec, vec, vec, vec,
                  pl.BlockSpec((1, VAL_DIM), lambda h, i: (0, 0))],
        out_specs=pl.BlockSpec((tq, VAL_DIM), lambda h, i: (i, h)),
        scratch_shapes=[pltpu.VMEM((2, VAL_DIM, tq), F32), pltpu.VMEM((2, 1, tq), F32),
                        pltpu.VMEM((2, 1, LANES), F32),
                        pltpu.VMEM((2, tk, tq), F32), pltpu.VMEM((2, tk, tq), F32),
                        pltpu.VMEM((2, tk, tq), BF16), pltpu.VMEM((2, tk, tq), BF16)],
        compiler_params=_cparams(("parallel", "arbitrary")),
        name="diff_attention",
    )(qt, k, vt, lq1, lk1, lq2, lk2, subln_g)


def _conv_kernel(x_ref, w_ref, b_ref, o_ref, *, rows, nchunks):
    w0 = w_ref[0:1, :]
    w1 = w_ref[1:2, :]
    w2 = w_ref[2:3, :]
    b = b_ref[...]
    grp = BF16_SUBLANES
    last = nchunks * rows - grp

    def body(i, carry):
        r0 = pl.multiple_of(i * rows, rows)
        x0 = x_ref[pl.ds(r0, rows), :].astype(F32)
        pr = x_ref[pl.ds(pl.multiple_of(jnp.maximum(r0 - grp, 0), grp), grp), :].astype(F32)
        nx = x_ref[pl.ds(pl.multiple_of(jnp.minimum(r0 + rows, last), grp), grp), :].astype(F32)
        prev_row = jnp.where(i > 0, pr[grp - 1:grp, :], 0.0)
        next_row = jnp.where(i < nchunks - 1, nx[0:1, :], 0.0)
        row = lax.broadcasted_iota(jnp.int32, x0.shape, 0)
        xm = jnp.where(row == 0, prev_row, pltpu.roll(x0, 1, 0))
        xp = jnp.where(row == rows - 1, next_row, pltpu.roll(x0, rows - 1, 0))
        o_ref[pl.ds(r0, rows), :] = (xm * w0 + x0 * w1 + xp * w2 + b).astype(o_ref.dtype)
        return carry

    lax.fori_loop(0, nchunks, body, 0)


def _short_conv(p, w, b, tn=256, rows=512):
    L, C = p.shape
    rows = min(rows, L)
    return pl.pallas_call(
        functools.partial(_conv_kernel, rows=rows, nchunks=L // rows),
        out_shape=jax.ShapeDtypeStruct((L, C), BF16),
        grid=(C // tn,),
        in_specs=[pl.BlockSpec((L, tn), lambda j: (0, j)),
                  pl.BlockSpec((3, tn), lambda j: (0, j)),
                  pl.BlockSpec((1, tn), lambda j: (0, j))],
        out_specs=pl.BlockSpec((L, tn), lambda j: (0, j)),
        compiler_params=_cparams(("parallel",)),
        name="short_conv",
    )(p, w, b)


def _split_bf16(a):
    hi = a.astype(BF16)
    return hi, (a - hi.astype(F32)).astype(BF16)


def _filter_kernel(z_ref, w1_ref, b1_ref, w2_ref, b2_ref, w3_ref, b3_ref, fr_ref, wfh_ref, wfl_ref,
                   wbh_ref, wbl_ref, dl_ref, bias_ref, o_ref, *, L, nh):
    hp = lax.Precision.HIGHEST
    fr = fr_ref[...]
    h = jnp.sin(fr * (jnp.dot(z_ref[...], w1_ref[...], precision=hp, preferred_element_type=F32) + b1_ref[...]))
    h = jnp.sin(fr * (jnp.dot(h, w2_ref[...], precision=hp, preferred_element_type=F32) + b2_ref[...]))
    h = jnp.sin(fr * (jnp.dot(h, w3_ref[...], precision=hp, preferred_element_type=F32) + b3_ref[...]))
    hh, hl = _split_bf16(h)

    def out_proj(wh_ref, wl_ref):
        wh = wh_ref[...]
        return (jnp.dot(hh, wh, preferred_element_type=F32) + jnp.dot(hl, wh, preferred_element_type=F32)
                + jnp.dot(hh, wl_ref[...], preferred_element_type=F32))

    rb = z_ref.shape[0]
    nq = rb // nh
    c2 = dl_ref.shape[1]
    row = lax.broadcasted_iota(jnp.int32, (rb, 1), 0)
    assert nh & (nh - 1) == 0
    r = jnp.bitwise_and(row, nh - 1)
    q_idx = lax.shift_right_logical(row, nh.bit_length() - 1)
    n = FFT_MINOR * r + nq * pl.program_id(0) + q_idx
    step = 1.0 / (L - 1)
    dl = dl_ref[...]
    kf = out_proj(wfh_ref, wfl_ref) * jnp.exp(-(n.astype(F32) * step) * dl)
    kf = (kf + jnp.where(n == 0, bias_ref[...], 0.0)).astype(o_ref.dtype)
    kb = out_proj(wbh_ref, wbl_ref) * jnp.exp(-((L - n).astype(F32) * step) * dl)
    kb = jnp.where(n == 0, 0.0, kb).astype(o_ref.dtype)
    for q in range(nq):
        o_ref[0:nh, q * c2:(q + 1) * c2] = kf[q * nh:(q + 1) * nh]
        o_ref[nh:2 * nh, q * c2:(q + 1) * c2] = kb[q * nh:(q + 1) * nh]


def _filter_features(L, nq):
    half = LANES // 2
    nh = L // FFT_MINOR
    bands = (FILTER_EMB - 1) // 2
    f = np.linspace(1e-4, bands - 1, bands)
    z = np.zeros((L, LANES))
    step, q, r = np.meshgrid(np.arange(FFT_MINOR // nq), np.arange(nq), np.arange(nh), indexing="ij")
    n = (FFT_MINOR * r + nq * step + q).reshape(-1).astype(np.float64)
    for off, pos in ((0, n), (half, L - n)):
        w = 2.0 * np.pi * pos / L
        z[:, off] = pos / (L - 1)
        z[:, off + 1:off + 1 + bands] = np.cos(f[None, :] * w[:, None])
        z[:, off + 1 + bands:off + 1 + 2 * bands] = -np.sin(f[None, :] * w[:, None])
    return jnp.asarray(z, F32)


def _pad_to(a, shape):
    return jnp.pad(a, [(0, s - d) for d, s in zip(a.shape, shape)])


def _block_diag2(a):
    half = LANES // 2
    p = _pad_to(a, (half, half))
    z = jnp.zeros_like(p)
    return jnp.concatenate([jnp.concatenate([p, z], axis=1), jnp.concatenate([z, p], axis=1)], axis=0)


def _hyena_filters(L, fw1, fb1, fw2, fb2, fw3, fb3, ffreq, fw_out, bias, nq=8):
    H = LANES
    half = H // 2
    C2 = 2 * D_HYENA
    nh = L // FFT_MINOR
    rb = nq * nh
    dup = lambda v: jnp.tile(_pad_to(v[None, :], (1, half)), (1, 2))
    wo = fw_out.reshape(FILTER_HIDDEN, 2, 2, D_HYENA)
    wf = _pad_to(wo[:, :, 0, :].reshape(FILTER_HIDDEN, C2), (H, C2))
    wb = jnp.concatenate([jnp.zeros((half, C2), F32), _pad_to(wo[:, :, 1, :].reshape(FILTER_HIDDEN, C2), (half, C2))])
    deltas = np.abs(np.linspace(MIN_DECAY, MAX_DECAY, D_HYENA))
    dl = jnp.asarray(np.tile(deltas, 2)[None, :], F32)
    full = lambda shape: pl.BlockSpec(shape, lambda i: (0,) * len(shape))
    return pl.pallas_call(
        functools.partial(_filter_kernel, L=L, nh=nh),
        out_shape=jax.ShapeDtypeStruct((2 * nh, FFT_MINOR * C2), BF16),
        grid=(L // rb,),
        in_specs=[pl.BlockSpec((rb, H), lambda i: (i, 0)),
                  full((H, H)), full((1, H)), full((H, H)), full((1, H)), full((H, H)), full((1, H)),
                  full((1, H)),
                  full((H, C2)), full((H, C2)), full((H, C2)), full((H, C2)),
                  full((1, C2)), full((1, C2))],
        out_specs=pl.BlockSpec((2 * nh, nq * C2), lambda i: (0, i)),
        compiler_params=_cparams(("parallel",)),
        name="hyena_filters",
    )(_filter_features(L, nq), _block_diag2(fw1), dup(fb1), _block_diag2(fw2), dup(fb2), _block_diag2(fw3),
      dup(fb3), dup(ffreq), *_split_bf16(wf), *_split_bf16(wb), dl, bias.reshape(1, C2))


def _dft_constants(L):
    N = 2 * L
    N1 = N // FFT_MINOR
    nh = N1 // 2
    k1h = N1 // 2 + 1
    k1p = -(-k1h // 8) * 8
    k1 = np.arange(k1h)[:, None]

    def stage_a(nn):
        th = 2 * np.pi * np.arange(nn)[None, :] * k1 / N1
        fa = np.zeros((2 * k1p, nn))
        fa[:k1h] = np.cos(th)
        fa[k1p:k1p + k1h] = -np.sin(th)
        return fa

    n2 = np.arange(FFT_MINOR)
    m1 = np.zeros((k1p, 2 * FFT_MINOR, 2 * FFT_MINOR))
    m2 = np.zeros_like(m1)
    for a in range(k1h):
        ph = -2 * np.pi * (n2[None, :] * a / N + n2[None, :] * n2[:, None] / FFT_MINOR)
        cr, ci = np.cos(ph), np.sin(ph)
        m1[a] = np.block([[cr, -ci], [ci, cr]])
        m2[a] = np.block([[cr.T, ci.T], [-ci.T, cr.T]])
    kk = np.arange(k1h)[None, :]
    cw = np.where((kk == 0) | (kk == N1 // 2), 1.0, 2.0) / N
    th = 2 * np.pi * np.arange(nh)[:, None] * kk / N1
    g = np.zeros((nh, 2 * k1p))
    g[:, :k1h] = cw * np.cos(th)
    g[:, k1p:k1p + k1h] = -cw * np.sin(th)
    as_bf16 = lambda a: jnp.asarray(a, F32).astype(BF16)
    return dict(nh=nh, n1=N1, k1h=k1h, k1p=k1p, fa_data=as_bf16(stage_a(nh)), fa_filt=as_bf16(stage_a(N1)),
                m1=as_bf16(m1), m2=as_bf16(m2), g=as_bf16(g))


def _lmat_kernel(f_ref, x_ref, o_ref):
    o_ref[...] = jnp.dot(f_ref[...], x_ref[...].astype(BF16), preferred_element_type=F32).astype(o_ref.dtype)


def _lmat(fmat, x2d, out_dtype=BF16, tn=16384):
    M, K = fmat.shape
    ncols = x2d.shape[1]
    tn = min(tn, ncols)
    return pl.pallas_call(
        _lmat_kernel,
        out_shape=jax.ShapeDtypeStruct((M, ncols), out_dtype),
        grid=(ncols // tn,),
        in_specs=[pl.BlockSpec((M, K), lambda j: (0, 0)),
                  pl.BlockSpec((K, tn), lambda j: (0, j))],
        out_specs=pl.BlockSpec((M, tn), lambda j: (0, j)),
        compiler_params=_cparams(("parallel",)),
        name="dft_stage_a",
    )(fmat, x2d)


def _lmat_gate_kernel(g_ref, z_ref, gate_ref, o_ref):
    o_ref[...] = jnp.dot(g_ref[...], z_ref[...], preferred_element_type=F32) * gate_ref[...]


def _lmat_gate(gmat, z2d, gate2d, tn=16384):
    M, K = gmat.shape
    ncols = z2d.shape[1]
    tn = min(tn, ncols)
    return pl.pallas_call(
        _lmat_gate_kernel,
        out_shape=jax.ShapeDtypeStruct((M, ncols), F32),
        grid=(ncols // tn,),
        in_specs=[pl.BlockSpec((M, K), lambda j: (0, 0)),
                  pl.BlockSpec((K, tn), lambda j: (0, j)),
                  pl.BlockSpec((M, tn), lambda j: (0, j))],
        out_specs=pl.BlockSpec((M, tn), lambda j: (0, j)),
        compiler_params=_cparams(("parallel",)),
        name="dft_stage_a_inv",
    )(gmat, z2d, gate2d)


K1_PER_STEP = 8


def _filter_spectrum_kernel(m1_ref, a_ref, o_ref):
    for b in range(m1_ref.shape[0]):
        x = jnp.concatenate([a_ref[0, b], a_ref[1, b]], axis=0)
        o_ref[b] = jnp.dot(m1_ref[b], x, preferred_element_type=F32).astype(o_ref.dtype)


def _filter_spectrum(m1, a4):
    _, k1p, n2, C = a4.shape
    kb = K1_PER_STEP
    return pl.pallas_call(
        _filter_spectrum_kernel,
        out_shape=jax.ShapeDtypeStruct((k1p, 2 * n2, C), BF16),
        grid=(k1p // kb,),
        in_specs=[pl.BlockSpec((kb, 2 * n2, 2 * n2), lambda a: (a, 0, 0)),
                  pl.BlockSpec((2, kb, n2, C), lambda a: (0, a, 0, 0))],
        out_specs=pl.BlockSpec((kb, 2 * n2, C), lambda a: (a, 0, 0)),
        compiler_params=_cparams(("parallel",)),
        name="filter_spectrum",
    )(m1, a4)


def _middle_kernel(m1_ref, m2_ref, a_ref, kf_ref, o_ref):
    n2 = a_ref.shape[2]
    for b in range(m1_ref.shape[0]):
        x = jnp.concatenate([a_ref[0, b], a_ref[1, b]], axis=0)
        s = jnp.dot(m1_ref[b], x, preferred_element_type=F32)
        sr, si = s[:n2], s[n2:]
        kf = kf_ref[b].astype(F32)
        kr, ki = kf[:n2], kf[n2:]
        y = jnp.concatenate([sr * kr - si * ki, sr * ki + si * kr], axis=0).astype(BF16)
        z = jnp.dot(m2_ref[b], y, preferred_element_type=F32).astype(o_ref.dtype)
        o_ref[0, b] = z[:n2]
        o_ref[1, b] = z[n2:]


def _middle(m1, m2, a4, kf, order):
    _, k1p, n2, C = a4.shape
    kb = K1_PER_STEP
    mat = pl.BlockSpec((kb, 2 * n2, 2 * n2), lambda a: (a, 0, 0))
    return pl.pallas_call(
        _middle_kernel,
        out_shape=jax.ShapeDtypeStruct((2, k1p, n2, C), BF16),
        grid=(k1p // kb,),
        in_specs=[mat, mat,
                  pl.BlockSpec((2, kb, n2, C), lambda a: (0, a, 0, 0)),
                  pl.BlockSpec((kb, 2 * n2, C), lambda a: (a, 0, order))],
        out_specs=pl.BlockSpec((2, kb, n2, C), lambda a: (0, a, 0, 0)),
        compiler_params=_cparams(("parallel",)),
        name="dft_middle",
    )(m1, m2, a4, kf)


def _hyena_long_convs(v, g1, kfilt2d, L):
    C = v.shape[1]
    cs = _dft_constants(L)
    nh, k1p = cs["nh"], cs["k1p"]
    fa = _lmat(cs["fa_filt"], kfilt2d)
    kspec = _filter_spectrum(cs["m1"], fa.reshape(2, k1p, FFT_MINOR, 2 * C))
    a = _lmat(cs["fa_data"], v.reshape(nh, FFT_MINOR * C))
    z = _middle(cs["m1"], cs["m2"], a.reshape(2, k1p, FFT_MINOR, C), kspec, 0)
    y2d = _lmat_gate(cs["g"], z.reshape(2 * k1p, FFT_MINOR * C), g1.reshape(nh, FFT_MINOR * C))
    a = _lmat(cs["fa_data"], y2d)
    z = _middle(cs["m1"], cs["m2"], a.reshape(2, k1p, FFT_MINOR, C), kspec, 1)
    return _lmat(cs["g"], z.reshape(2 * k1p, FFT_MINOR * C), F32).reshape(L, C)


def _out_kernel(yh_ref, gate_ref, ya_ref, wh_ref, wa_ref, ng_ref, x_ref, g_ref, b_ref, o_ref):
    yh = yh_ref[...] * gate_ref[...]
    yh = yh * lax.rsqrt(jnp.mean(yh * yh, axis=-1, keepdims=True) + LN_EPS) * ng_ref[...]
    acc = jnp.dot(yh.astype(BF16), wh_ref[...], preferred_element_type=F32)
    acc = acc + jnp.dot(ya_ref[...], wa_ref[...], preferred_element_type=F32)
    o_ref[...] = _layer_norm(ALPHA * x_ref[...] + acc, g_ref[...], b_ref[...])


def _out_ln(y_hy, z, gate_block, y_at, w_out, norm_g, x1, g, b, tm=512):
    L, D = x1.shape
    ch, ca = y_hy.shape[1], y_at.shape[1]
    tm = min(tm, L)
    row = lambda c: pl.BlockSpec((tm, c), lambda i: (i, 0))
    full = lambda r, c: pl.BlockSpec((r, c), lambda i: (0, 0))
    return pl.pallas_call(
        _out_kernel,
        out_shape=jax.ShapeDtypeStruct((L, D), F32),
        grid=(L // tm,),
        in_specs=[row(ch), pl.BlockSpec((tm, ch), lambda i: (i, gate_block)), row(ca),
                  pl.BlockSpec((ch, D), lambda i: (0, 0)),
                  pl.BlockSpec((ca, D), lambda i: (1, 0)),
                  full(1, ch), row(D), full(1, D), full(1, D)],
        out_specs=row(D),
        compiler_params=_cparams(("parallel",)),
        name="out_ln",
    )(y_hy, z, y_at, w_out, w_out, norm_g, x1, g, b)


def kernel(x, ffn1_w_gate, ffn1_w_up, ffn1_w_down, ln1_g, ln1_b, w_in, hyena_conv_w, hyena_conv_b, filt_w1, filt_b1, filt_w2, filt_b2, filt_w3, filt_b3, filt_freq, filt_w_out, hyena_bias, hyena_norm_g, lambda_q1, lambda_k1, lambda_q2, lambda_k2, subln_g, w_out, ln2_g, ln2_b, ffn2_w_gate, ffn2_w_up, ffn2_w_down, ln3_g, ln3_b):
    assert x.shape[0] == 1 and ffn1_w_gate.shape[0] == DEPTH
    L = x.shape[1]
    bf = lambda a: a.astype(BF16)
    row = lambda a: a.reshape(1, -1)
    h = x[0]
    s_hy = 3 * D_HYENA
    for i in range(DEPTH):
        h = _ffn_ln(h, bf(ffn1_w_gate[i]), bf(ffn1_w_up[i]), bf(ffn1_w_down[i]), row(ln1_g[i]), row(ln1_b[i]))
        w_i = w_in[i]
        p_hy = _proj(h, bf(w_i[:, :s_hy]), BF16)
        q_t = _proj_q_t(h, bf(w_i[:, s_hy:s_hy + D_QK].T))
        k_at = _proj_rope(h, bf(w_i[:, s_hy + D_QK:s_hy + 2 * D_QK]))
        v_t = _proj_v_t(h, bf(w_i[:, s_hy + 2 * D_QK:].T), ATTN_TK)
        z = _short_conv(p_hy, hyena_conv_w[i], row(hyena_conv_b[i]))
        kfilt = _hyena_filters(L, filt_w1[i], filt_b1[i], filt_w2[i], filt_b2[i], filt_w3[i], filt_b3[i],
                               filt_freq[i], filt_w_out[i], hyena_bias[i])
        y_hy = _hyena_long_convs(z[:, :D_HYENA], z[:, D_HYENA:2 * D_HYENA], kfilt, L)
        y_at = _diff_attention(q_t, k_at, v_t, row(lambda_q1[i]), row(lambda_k1[i]), row(lambda_q2[i]),
                               row(lambda_k2[i]), row(subln_g[i]))
        h = _out_ln(y_hy, z, 2, y_at, bf(w_out[i]), row(hyena_norm_g[i]), h, row(ln2_g[i]), row(ln2_b[i]))
        h = _ffn_ln(h, bf(ffn2_w_gate[i]), bf(ffn2_w_up[i]), bf(ffn2_w_down[i]), row(ln3_g[i]), row(ln3_b[i]))
    return h[None]
```

```python
import functools
import math

import numpy as np
import jax
import jax.numpy as jnp
from jax import lax
from jax.experimental import pallas as pl
from jax.experimental.pallas import tpu as pltpu

F32 = jnp.float32
BF16 = jnp.bfloat16

D_HYENA = 1024
N_HEADS = 8
HEAD_DIM = 64
VAL_DIM = 128
D_QK = 2 * N_HEADS * HEAD_DIM
ROT_DIM = 16
ROPE_THETA = 500000.0
FILTER_EMB = 33
FILTER_HIDDEN = 64
DECAY_TARGET = 1e-2
MIN_DECAY = math.log(DECAY_TARGET) / 1.5
MAX_DECAY = math.log(DECAY_TARGET) / 0.3
LN_EPS = 1e-5
DEPTH = 1
ALPHA = (2.0 * DEPTH) ** 0.25
LAMBDA_INIT = 0.8 - 0.6 * math.exp(-0.3 * 0)
LOG2E = math.log2(math.e)

LANES = 128
BF16_SUBLANES = 16
FFT_MINOR = 128
VMEM_LIMIT = 56 * 1024 * 1024
ATTN_TQ = 512
ATTN_TK = 1024
FIXED_SHIFT_MIN_DENOM = 2.0 ** -60


def _cparams(sem):
    return pltpu.CompilerParams(dimension_semantics=sem, vmem_limit_bytes=VMEM_LIMIT)


def _layer_norm(y, g, b):
    mu = jnp.mean(y, axis=-1, keepdims=True)
    d = y - mu
    var = jnp.mean(d * d, axis=-1, keepdims=True)
    return d * lax.rsqrt(var + LN_EPS) * g + b


def _ffn_kernel(x_ref, wg_ref, wu_ref, wd_ref, g_ref, b_ref, o_ref, xb_ref, acc_ref, h_ref, *, nf):
    f = pl.program_id(1)

    def up(slot):
        xb = xb_ref[...]
        hg = jnp.dot(xb, wg_ref[...], preferred_element_type=F32)
        hu = jnp.dot(xb, wu_ref[...], preferred_element_type=F32)
        h_ref[slot] = (hg * jax.nn.sigmoid(hg) * hu).astype(BF16)

    def down(slot):
        acc_ref[...] += jnp.dot(h_ref[slot], wd_ref[...], preferred_element_type=F32)

    @pl.when(f == 0)
    def _():
        xb_ref[...] = x_ref[...].astype(BF16)
        acc_ref[...] = jnp.zeros_like(acc_ref)
        up(0)

    @pl.when(jnp.logical_and(f > 0, f < nf))
    def _():
        down((f - 1) % 2)
        up(f % 2)

    @pl.when(f == nf)
    def _():
        down((nf - 1) % 2)
        y = ALPHA * x_ref[...] + 0.5 * acc_ref[...]
        o_ref[...] = _layer_norm(y, g_ref[...], b_ref[...])


def _ffn_ln(x, wg, wu, wd, g, b, tm=1024, tf=512):
    L, D = x.shape
    nf = wg.shape[1] // tf
    tm = min(tm, L)
    up_w = pl.BlockSpec((D, tf), lambda i, f: (0, jnp.minimum(f, nf - 1)))
    once = dict(pipeline_mode=pl.Buffered(1))
    return pl.pallas_call(
        functools.partial(_ffn_kernel, nf=nf),
        out_shape=jax.ShapeDtypeStruct((L, D), F32),
        grid=(L // tm, nf + 1),
        in_specs=[
            pl.BlockSpec((tm, D), lambda i, f: (i, 0), **once),
            up_w, up_w,
            pl.BlockSpec((tf, D), lambda i, f: (jnp.maximum(f - 1, 0), 0)),
            pl.BlockSpec((1, D), lambda i, f: (0, 0)),
            pl.BlockSpec((1, D), lambda i, f: (0, 0)),
        ],
        out_specs=pl.BlockSpec((tm, D), lambda i, f: (i, 0), **once),
        scratch_shapes=[pltpu.VMEM((tm, D), BF16), pltpu.VMEM((tm, D), F32), pltpu.VMEM((2, tm, tf), BF16)],
        compiler_params=_cparams(("parallel", "arbitrary")),
        name="ffn_ln",
    )(x, wg, wu, wd, g, b)


def _proj_kernel(x_ref, w_ref, o_ref, xb_ref):
    @pl.when(pl.program_id(1) == 0)
    def _():
        xb_ref[...] = x_ref[...].astype(BF16)

    o_ref[...] = jnp.dot(xb_ref[...], w_ref[...], preferred_element_type=F32).astype(o_ref.dtype)


def _proj(x, w, out_dtype, tm=1024, tn=1024):
    L, D = x.shape
    N = w.shape[1]
    tm = min(tm, L)
    return pl.pallas_call(
        _proj_kernel,
        out_shape=jax.ShapeDtypeStruct((L, N), out_dtype),
        grid=(L // tm, N // tn),
        in_specs=[pl.BlockSpec((tm, D), lambda i, j: (i, 0)),
                  pl.BlockSpec((D, tn), lambda i, j: (0, j))],
        out_specs=pl.BlockSpec((tm, tn), lambda i, j: (i, j)),
        scratch_shapes=[pltpu.VMEM((tm, D), BF16)],
        compiler_params=_cparams(("parallel", "arbitrary")),
        name="proj",
    )(x, w)


def _rope_tables(L, scale):
    half = ROT_DIM // 2
    inv = ROPE_THETA ** (-np.arange(0, ROT_DIM, 2, dtype=np.float64) / ROT_DIM)
    pos = np.arange(L, dtype=np.float64)[:, None]
    d = np.arange(LANES) % HEAD_DIM
    ang = pos * inv[d % half][None, :]
    c = np.where(d[None, :] < ROT_DIM, np.cos(ang), 1.0) * scale
    s1 = np.where(d[None, :] < half, -np.sin(ang), 0.0) * scale
    s2 = np.where((d[None, :] >= half) & (d[None, :] < ROT_DIM), np.sin(ang), 0.0) * scale
    return c, s1, s2


def _proj_rope_kernel(x_ref, w_ref, c_ref, s1_ref, s2_ref, o_ref, xb_ref, *, tn):
    @pl.when(pl.program_id(1) == 0)
    def _():
        xb_ref[...] = x_ref[...].astype(BF16)

    p = jnp.dot(xb_ref[...], w_ref[...], preferred_element_type=F32)
    c, s1, s2 = c_ref[...], s1_ref[...], s2_ref[...]
    shift = ROT_DIM // 2
    for h in range(tn // LANES):
        ph = p[:, h * LANES:(h + 1) * LANES]
        r = ph * c + pltpu.roll(ph, LANES - shift, 1) * s1 + pltpu.roll(ph, shift, 1) * s2
        o_ref[:, h * LANES:(h + 1) * LANES] = r.astype(o_ref.dtype)


def _proj_rope(x, w, tm=1024, tn=1024):
    L, D = x.shape
    N = w.shape[1]
    tm = min(tm, L)
    tabs = [jnp.asarray(t, F32) for t in _rope_tables(L, 1.0)]
    tab = pl.BlockSpec((tm, LANES), lambda i, j: (i, 0))
    return pl.pallas_call(
        functools.partial(_proj_rope_kernel, tn=tn),
        out_shape=jax.ShapeDtypeStruct((L, N), BF16),
        grid=(L // tm, N // tn),
        in_specs=[pl.BlockSpec((tm, D), lambda i, j: (i, 0)),
                  pl.BlockSpec((D, tn), lambda i, j: (0, j)),
                  tab, tab, tab],
        out_specs=pl.BlockSpec((tm, tn), lambda i, j: (i, j)),
        scratch_shapes=[pltpu.VMEM((tm, D), BF16)],
        compiler_params=_cparams(("parallel", "arbitrary")),
        name="proj_rope",
    )(x, w, *tabs)


def _projT_kernel(x_ref, w_ref, *rest, rope, tn, tk):
    if rope:
        c_ref, s1_ref, s2_ref, o_ref, xb_ref = rest
    else:
        o_ref, xb_ref = rest

    @pl.when(pl.program_id(1) == 0)
    def _():
        xb_ref[...] = x_ref[...].astype(BF16)

    pt = jnp.dot(xb_ref[...], w_ref[...], preferred_element_type=F32).T
    if rope:
        c, s1, s2 = c_ref[...], s1_ref[...], s2_ref[...]
        shift = ROT_DIM // 2
        for h in range(tn // LANES):
            ph = pt[h * LANES:(h + 1) * LANES, :]
            r = ph * c + pltpu.roll(ph, LANES - shift, 0) * s1 + pltpu.roll(ph, shift, 0) * s2
            o_ref[h * LANES:(h + 1) * LANES, :] = r.astype(o_ref.dtype)
    else:
        for s in range(pt.shape[1] // tk):
            o_ref[s] = pt[:, s * tk:(s + 1) * tk].astype(o_ref.dtype)


def _proj_q_t(x, w, tm=1024, tn=512):
    L, D = x.shape
    N = w.shape[1]
    tm = min(tm, L)
    tabs = [jnp.asarray(t.T, F32) for t in _rope_tables(L, HEAD_DIM ** -0.5 * LOG2E)]
    tab = pl.BlockSpec((LANES, tm), lambda i, j: (0, i))
    return pl.pallas_call(
        functools.partial(_projT_kernel, rope=True, tn=tn, tk=None),
        out_shape=jax.ShapeDtypeStruct((N, L), BF16),
        grid=(L // tm, N // tn),
        in_specs=[pl.BlockSpec((tm, D), lambda i, j: (i, 0)),
                  pl.BlockSpec((D, tn), lambda i, j: (0, j)),
                  tab, tab, tab],
        out_specs=pl.BlockSpec((tn, tm), lambda i, j: (j, i)),
        scratch_shapes=[pltpu.VMEM((tm, D), BF16)],
        compiler_params=_cparams(("parallel", "arbitrary")),
        name="proj_q_t",
    )(x, w, *tabs)


def _proj_v_t(x, w, tk, tm=1024, tn=512):
    L, D = x.shape
    N = w.shape[1]
    tm = min(tm, L)
    tk = min(tk, L)
    per = tm // tk
    return pl.pallas_call(
        functools.partial(_projT_kernel, rope=False, tn=tn, tk=tk),
        out_shape=jax.ShapeDtypeStruct((L // tk, N, tk), BF16),
        grid=(L // tm, N // tn),
        in_specs=[pl.BlockSpec((tm, D), lambda i, j: (i, 0)),
                  pl.BlockSpec((D, tn), lambda i, j: (0, j))],
        out_specs=pl.BlockSpec((per, tn, tk), lambda i, j: (i, j, 0)),
        scratch_shapes=[pltpu.VMEM((tm, D), BF16)],
        compiler_params=_cparams(("parallel", "arbitrary")),
        name="proj_v_t",
    )(x, w)


def _attn_kernel(qt_ref, k_ref, vt_ref, lq1_ref, lk1_ref, lq2_ref, lk2_ref, g_ref, o_ref,
                 acc_ref, l_ref, kn_ref, sa_ref, sb_ref, pa_ref, pb_ref, *, tk, nk):
    qt = qt_ref[...].astype(F32)
    tq = qt.shape[1]
    row = lax.broadcasted_iota(jnp.int32, qt.shape, 0)
    qs = (jnp.where(row < HEAD_DIM, qt, 0.0).astype(BF16),
          jnp.where(row >= HEAD_DIM, qt, 0.0).astype(BF16))
    zero = jnp.zeros((1, tq), F32)

    def k_chunk(j):
        return k_ref[pl.ds(pl.multiple_of(j * tk, tk), tk), :]

    @pl.when(pl.program_id(1) == 0)
    def _():
        def kbody(c, carry):
            kb = k_chunk(c).astype(F32)
            sq = kb * kb
            lane = lax.broadcasted_iota(jnp.int32, sq.shape, 1)
            n0 = jnp.sum(jnp.where(lane < HEAD_DIM, sq, 0.0), axis=1, keepdims=True)
            n1 = jnp.sum(jnp.where(lane >= HEAD_DIM, sq, 0.0), axis=1, keepdims=True)
            return (jnp.maximum(carry[0], jnp.max(n0, axis=0, keepdims=True)),
                    jnp.maximum(carry[1], jnp.max(n1, axis=0, keepdims=True)))

        z = jnp.zeros((1, 1), F32)
        k0, k1 = lax.fori_loop(0, nk, kbody, (z, z))
        kn_ref[0] = jnp.broadcast_to(k0, (1, LANES))
        kn_ref[1] = jnp.broadcast_to(k1, (1, LANES))

    q2 = qt * qt
    qn = (jnp.sum(q2[:HEAD_DIM], axis=0, keepdims=True), jnp.sum(q2[HEAD_DIM:], axis=0, keepdims=True))
    mb = tuple(jnp.sqrt(qn[mi] * kn_ref[mi][:, 0:1]) for mi in range(2))
    acc_ref[...] = jnp.zeros_like(acc_ref)

    def fixed_probs(j, p_ref):
        kb = k_chunk(j)
        sums = []
        for mi in range(2):
            pt = jnp.exp2(jnp.dot(kb, qs[mi], preferred_element_type=F32) - mb[mi])
            p_ref[mi] = pt.astype(BF16)
            sums.append(jnp.sum(pt, axis=0, keepdims=True))
        return tuple(sums)

    def fixed_accumulate(j, p_ref):
        vb = vt_ref[j]
        for mi in range(2):
            acc_ref[mi] += jnp.dot(vb, p_ref[mi], preferred_element_type=F32)

    def fixed_body(jj, carry):
        ls, la = carry
        j = 2 * jj
        lb = fixed_probs(j + 1, pb_ref)
        fixed_accumulate(j, pa_ref)
        ls = (ls[0] + la[0] + lb[0], ls[1] + la[1] + lb[1])
        la = fixed_probs(j + 2, pa_ref)
        fixed_accumulate(j + 1, pb_ref)
        return ls, la

    ls, la = lax.fori_loop(0, nk // 2 - 1, fixed_body, ((zero, zero), fixed_probs(0, pa_ref)))
    lb = fixed_probs(nk - 1, pb_ref)
    fixed_accumulate(nk - 2, pa_ref)
    fixed_accumulate(nk - 1, pb_ref)
    ls = (ls[0] + la[0] + lb[0], ls[1] + la[1] + lb[1])
    l_ref[0] = ls[0]
    l_ref[1] = ls[1]
    trusted = jnp.min(jnp.minimum(ls[0], ls[1])) >= FIXED_SHIFT_MIN_DENOM

    @pl.when(jnp.logical_not(trusted))
    def _():
        acc_ref[...] = jnp.zeros_like(acc_ref)
        neg = jnp.full((1, tq), -jnp.inf, F32)

        def scores(j, s_ref):
            kb = k_chunk(j)
            mx = []
            for mi in range(2):
                st = jnp.dot(kb, qs[mi], preferred_element_type=F32)
                s_ref[mi] = st
                mx.append(jnp.max(st, axis=0, keepdims=True))
            return tuple(mx)

        def consume(j, s_ref, mx, ms, lr):
            vb = vt_ref[j]
            new_m, new_l = [], []
            for mi in range(2):
                m_next = jnp.maximum(ms[mi], mx[mi])
                alpha = jnp.exp2(ms[mi] - m_next)
                pt = jnp.exp2(s_ref[mi] - m_next)
                new_l.append(alpha * lr[mi] + jnp.sum(pt, axis=0, keepdims=True))
                new_m.append(m_next)
                acc_ref[mi] = acc_ref[mi] * alpha + jnp.dot(vb, pt.astype(BF16), preferred_element_type=F32)
            return tuple(new_m), tuple(new_l)

        def body(jj, carry):
            ms, lr, mxa = carry
            j = 2 * jj
            mxb = scores(j + 1, sb_ref)
            ms, lr = consume(j, sa_ref, mxa, ms, lr)
            mxa = scores(jnp.minimum(j + 2, nk - 1), sa_ref)
            ms, lr = consume(j + 1, sb_ref, mxb, ms, lr)
            return ms, lr, mxa

        _, lr, _ = lax.fori_loop(0, nk // 2, body, ((neg, neg), (zero, zero), scores(0, sa_ref)))
        l_ref[0] = lr[0]
        l_ref[1] = lr[1]

    lam = (jnp.exp(jnp.sum(lq1_ref[...] * lk1_ref[...], axis=-1, keepdims=True))
           - jnp.exp(jnp.sum(lq2_ref[...] * lk2_ref[...], axis=-1, keepdims=True)) + LAMBDA_INIT)
    ot = acc_ref[0] / l_ref[0] - lam * (acc_ref[1] / l_ref[1])
    ot = ot * lax.rsqrt(jnp.mean(ot * ot, axis=0, keepdims=True) + LN_EPS)
    o_ref[...] = (ot.T * (g_ref[...] * (1.0 - LAMBDA_INIT))).astype(o_ref.dtype)


def _diff_attention(qt, k, vt, lq1, lk1, lq2, lk2, subln_g, tq=ATTN_TQ):
    L = k.shape[0]
    nk, _, tk = vt.shape
    assert nk % 2 == 0, "the kv loop handles chunks in pairs"
    tq = min(tq, L)
    vec = pl.BlockSpec((1, HEAD_DIM), lambda h, i: (0, 0))
    return pl.pallas_call(
        functools.partial(_attn_kernel, tk=tk, nk=nk),
        out_shape=jax.ShapeDtypeStruct((L, N_HEADS * VAL_DIM), BF16),
        grid=(N_HEADS, L // tq),
        in_specs=[pl.BlockSpec((LANES, tq), lambda h, i: (h, i)),
                  pl.BlockSpec((L, LANES), lambda h, i: (0, h)),
                  pl.BlockSpec((nk, VAL_DIM, tk), lambda h, i: (0, h, 0)),
                  vec, vec, vec, vec,
                  pl.BlockSpec((1, VAL_DIM), lambda h, i: (0, 0))],
        out_specs=pl.BlockSpec((tq, VAL_DIM), lambda h, i: (i, h)),
        scratch_shapes=[pltpu.VMEM((2, VAL_DIM, tq), F32), pltpu.VMEM((2, 1, tq), F32),
                        pltpu.VMEM((2, 1, LANES), F32),
                        pltpu.VMEM((2, tk, tq), F32), pltpu.VMEM((2, tk, tq), F32),
                        pltpu.VMEM((2, tk, tq), BF16), pltpu.VMEM((2, tk, tq), BF16)],
        compiler_params=_cparams(("parallel", "arbitrary")),
        name="diff_attention",
    )(qt, k, vt, lq1, lk1, lq2, lk2, subln_g)


def _conv_kernel(x_ref, w_ref, b_ref, o_ref, *, rows, nchunks):
    w0 = w_ref[0:1, :]
    w1 = w_ref[1:2, :]
    w2 = w_ref[2:3, :]
    b = b_ref[...]
    grp = BF16_SUBLANES
    last = nchunks * rows - grp

    def body(i, carry):
        r0 = pl.multiple_of(i * rows, rows)
        x0 = x_ref[pl.ds(r0, rows), :].astype(F32)
        pr = x_ref[pl.ds(pl.multiple_of(jnp.maximum(r0 - grp, 0), grp), grp), :].astype(F32)
        nx = x_ref[pl.ds(pl.multiple_of(jnp.minimum(r0 + rows, last), grp), grp), :].astype(F32)
        prev_row = jnp.where(i > 0, pr[grp - 1:grp, :], 0.0)
        next_row = jnp.where(i < nchunks - 1, nx[0:1, :], 0.0)
        row = lax.broadcasted_iota(jnp.int32, x0.shape, 0)
        xm = jnp.where(row == 0, prev_row, pltpu.roll(x0, 1, 0))
        xp = jnp.where(row == rows - 1, next_row, pltpu.roll(x0, rows - 1, 0))
        o_ref[pl.ds(r0, rows), :] = (xm * w0 + x0 * w1 + xp * w2 + b).astype(o_ref.dtype)
        return carry

    lax.fori_loop(0, nchunks, body, 0)


def _short_conv(p, w, b, tn=256, rows=512):
    L, C = p.shape
    rows = min(rows, L)
    return pl.pallas_call(
        functools.partial(_conv_kernel, rows=rows, nchunks=L // rows),
        out_shape=jax.ShapeDtypeStruct((L, C), BF16),
        grid=(C // tn,),
        in_specs=[pl.BlockSpec((L, tn), lambda j: (0, j)),
                  pl.BlockSpec((3, tn), lambda j: (0, j)),
                  pl.BlockSpec((1, tn), lambda j: (0, j))],
        out_specs=pl.BlockSpec((L, tn), lambda j: (0, j)),
        compiler_params=_cparams(("parallel",)),
        name="short_conv",
    )(p, w, b)


def _filter_kernel(z_ref, w1_ref, b1_ref, w2_ref, b2_ref, w3_ref, b3_ref, fr_ref, wf_ref, wb_ref,
                   dl_ref, bias_ref, o_ref, *, L, nh):
    hp = lax.Precision.HIGHEST
    fr = fr_ref[...]
    h = jnp.sin(fr * (jnp.dot(z_ref[...], w1_ref[...], precision=hp, preferred_element_type=F32) + b1_ref[...]))
    h = jnp.sin(fr * (jnp.dot(h, w2_ref[...], precision=hp, preferred_element_type=F32) + b2_ref[...]))
    h = jnp.sin(fr * (jnp.dot(h, w3_ref[...], precision=hp, preferred_element_type=F32) + b3_ref[...]))
    hb = h.astype(BF16)

    def out_proj(w_ref):
        return jnp.dot(hb, w_ref[...], preferred_element_type=F32)

    rb = z_ref.shape[0]
    nq = rb // nh
    c2 = dl_ref.shape[1]
    row = lax.broadcasted_iota(jnp.int32, (rb, 1), 0)
    assert nh & (nh - 1) == 0
    r = jnp.bitwise_and(row, nh - 1)
    q_idx = lax.shift_right_logical(row, nh.bit_length() - 1)
    n = FFT_MINOR * r + nq * pl.program_id(0) + q_idx
    step = 1.0 / (L - 1)
    dl = dl_ref[...]
    kf = out_proj(wf_ref) * jnp.exp(-(n.astype(F32) * step) * dl)
    kf = (kf + jnp.where(n == 0, bias_ref[...], 0.0)).astype(o_ref.dtype)
    kb = out_proj(wb_ref) * jnp.exp(-((L - n).astype(F32) * step) * dl)
    kb = jnp.where(n == 0, 0.0, kb).astype(o_ref.dtype)
    for q in range(nq):
        o_ref[0:nh, q * c2:(q + 1) * c2] = kf[q * nh:(q + 1) * nh]
        o_ref[nh:2 * nh, q * c2:(q + 1) * c2] = kb[q * nh:(q + 1) * nh]


def _filter_features(L, nq):
    half = LANES // 2
    nh = L // FFT_MINOR
    bands = (FILTER_EMB - 1) // 2
    f = np.linspace(1e-4, bands - 1, bands)
    z = np.zeros((L, LANES))
    step, q, r = np.meshgrid(np.arange(FFT_MINOR // nq), np.arange(nq), np.arange(nh), indexing="ij")
    n = (FFT_MINOR * r + nq * step + q).reshape(-1).astype(np.float64)
    for off, pos in ((0, n), (half, L - n)):
        w = 2.0 * np.pi * pos / L
        z[:, off] = pos / (L - 1)
        z[:, off + 1:off + 1 + bands] = np.cos(f[None, :] * w[:, None])
        z[:, off + 1 + bands:off + 1 + 2 * bands] = -np.sin(f[None, :] * w[:, None])
    return jnp.asarray(z, F32)


def _pad_to(a, shape):
    return jnp.pad(a, [(0, s - d) for d, s in zip(a.shape, shape)])


def _block_diag2(a):
    half = LANES // 2
    p = _pad_to(a, (half, half))
    z = jnp.zeros_like(p)
    return jnp.concatenate([jnp.concatenate([p, z], axis=1), jnp.concatenate([z, p], axis=1)], axis=0)


def _hyena_filters(L, fw1, fb1, fw2, fb2, fw3, fb3, ffreq, fw_out, bias, nq=8):
    H = LANES
    half = H // 2
    C2 = 2 * D_HYENA
    nh = L // FFT_MINOR
    rb = nq * nh
    dup = lambda v: jnp.tile(_pad_to(v[None, :], (1, half)), (1, 2))
    wo = fw_out.reshape(FILTER_HIDDEN, 2, 2, D_HYENA)
    wf = _pad_to(wo[:, :, 0, :].reshape(FILTER_HIDDEN, C2), (H, C2))
    wb = jnp.concatenate([jnp.zeros((half, C2), F32), _pad_to(wo[:, :, 1, :].reshape(FILTER_HIDDEN, C2), (half, C2))])
    deltas = np.abs(np.linspace(MIN_DECAY, MAX_DECAY, D_HYENA))
    dl = jnp.asarray(np.tile(deltas, 2)[None, :], F32)
    full = lambda shape: pl.BlockSpec(shape, lambda i: (0,) * len(shape))
    return pl.pallas_call(
        functools.partial(_filter_kernel, L=L, nh=nh),
        out_shape=jax.ShapeDtypeStruct((2 * nh, FFT_MINOR * C2), BF16),
        grid=(L // rb,),
        in_specs=[pl.BlockSpec((rb, H), lambda i: (i, 0)),
                  full((H, H)), full((1, H)), full((H, H)), full((1, H)), full((H, H)), full((1, H)),
                  full((1, H)),
                  full((H, C2)), full((H, C2)),
                  full((1, C2)), full((1, C2))],
        out_specs=pl.BlockSpec((2 * nh, nq * C2), lambda i: (0, i)),
        compiler_params=_cparams(("parallel",)),
        name="hyena_filters",
    )(_filter_features(L, nq), _block_diag2(fw1), dup(fb1), _block_diag2(fw2), dup(fb2), _block_diag2(fw3),
      dup(fb3), dup(ffreq), wf.astype(BF16), wb.astype(BF16), dl, bias.reshape(1, C2))


def _dft_constants(L):
    N = 2 * L
    N1 = N // FFT_MINOR
    nh = N1 // 2
    k1h = N1 // 2 + 1
    k1p = -(-k1h // 8) * 8
    k1 = np.arange(k1h)[:, None]

    def stage_a(nn):
        th = 2 * np.pi * np.arange(nn)[None, :] * k1 / N1
        fa = np.zeros((2 * k1p, nn))
        fa[:k1h] = np.cos(th)
        fa[k1p:k1p + k1h] = -np.sin(th)
        return fa

    n2 = np.arange(FFT_MINOR)
    m1 = np.zeros((k1p, 2 * FFT_MINOR, 2 * FFT_MINOR))
    m2 = np.zeros_like(m1)
    for a in range(k1h):
        ph = -2 * np.pi * (n2[None, :] * a / N + n2[None, :] * n2[:, None] / FFT_MINOR)
        cr, ci = np.cos(ph), np.sin(ph)
        m1[a] = np.block([[cr, -ci], [ci, cr]])
        m2[a] = np.block([[cr.T, ci.T], [-ci.T, cr.T]])
    kk = np.arange(k1h)[None, :]
    cw = np.where((kk == 0) | (kk == N1 // 2), 1.0, 2.0) / N
    th = 2 * np.pi * np.arange(nh)[:, None] * kk / N1
    g = np.zeros((nh, 2 * k1p))
    g[:, :k1h] = cw * np.cos(th)
    g[:, k1p:k1p + k1h] = -cw * np.sin(th)
    as_bf16 = lambda a: jnp.asarray(a, F32).astype(BF16)
    return dict(nh=nh, n1=N1, k1h=k1h, k1p=k1p, fa_data=as_bf16(stage_a(nh)), fa_filt=as_bf16(stage_a(N1)),
                m1=as_bf16(m1), m2=as_bf16(m2), g=as_bf16(g))


def _lmat_kernel(f_ref, x_ref, o_ref):
    o_ref[...] = jnp.dot(f_ref[...], x_ref[...].astype(BF16), preferred_element_type=F32).astype(o_ref.dtype)


def _lmat(fmat, x2d, out_dtype=BF16, tn=16384):
    M, K = fmat.shape
    ncols = x2d.shape[1]
    tn = min(tn, ncols)
    return pl.pallas_call(
        _lmat_kernel,
        out_shape=jax.ShapeDtypeStruct((M, ncols), out_dtype),
        grid=(ncols // tn,),
        in_specs=[pl.BlockSpec((M, K), lambda j: (0, 0)),
                  pl.BlockSpec((K, tn), lambda j: (0, j))],
        out_specs=pl.BlockSpec((M, tn), lambda j: (0, j)),
        compiler_params=_cparams(("parallel",)),
        name="dft_stage_a",
    )(fmat, x2d)


def _lmat_gate_kernel(g_ref, z_ref, gate_ref, o_ref):
    o_ref[...] = jnp.dot(g_ref[...], z_ref[...], preferred_element_type=F32) * gate_ref[...]


def _lmat_gate(gmat, z2d, gate2d, tn=16384):
    M, K = gmat.shape
    ncols = z2d.shape[1]
    tn = min(tn, ncols)
    return pl.pallas_call(
        _lmat_gate_kernel,
        out_shape=jax.ShapeDtypeStruct((M, ncols), F32),
        grid=(ncols // tn,),
        in_specs=[pl.BlockSpec((M, K), lambda j: (0, 0)),
                  pl.BlockSpec((K, tn), lambda j: (0, j)),
                  pl.BlockSpec((M, tn), lambda j: (0, j))],
        out_specs=pl.BlockSpec((M, tn), lambda j: (0, j)),
        compiler_params=_cparams(("parallel",)),
        name="dft_stage_a_inv",
    )(gmat, z2d, gate2d)


K1_PER_STEP = 8


def _filter_spectrum_kernel(m1_ref, a_ref, o_ref):
    for b in range(m1_ref.shape[0]):
        x = jnp.concatenate([a_ref[0, b], a_ref[1, b]], axis=0)
        o_ref[b] = jnp.dot(m1_ref[b], x, preferred_element_type=F32).astype(o_ref.dtype)


def _filter_spectrum(m1, a4):
    _, k1p, n2, C = a4.shape
    kb = K1_PER_STEP
    return pl.pallas_call(
        _filter_spectrum_kernel,
        out_shape=jax.ShapeDtypeStruct((k1p, 2 * n2, C), BF16),
        grid=(k1p // kb,),
        in_specs=[pl.BlockSpec((kb, 2 * n2, 2 * n2), lambda a: (a, 0, 0)),
                  pl.BlockSpec((2, kb, n2, C), lambda a: (0, a, 0, 0))],
        out_specs=pl.BlockSpec((kb, 2 * n2, C), lambda a: (a, 0, 0)),
        compiler_params=_cparams(("parallel",)),
        name="filter_spectrum",
    )(m1, a4)


def _middle_kernel(m1_ref, m2_ref, a_ref, kf_ref, o_ref):
    n2 = a_ref.shape[2]
    for b in range(m1_ref.shape[0]):
        x = jnp.concatenate([a_ref[0, b], a_ref[1, b]], axis=0)
        s = jnp.dot(m1_ref[b], x, preferred_element_type=F32)
        sr, si = s[:n2], s[n2:]
        kf = kf_ref[b].astype(F32)
        kr, ki = kf[:n2], kf[n2:]
        y = jnp.concatenate([sr * kr - si * ki, sr * ki + si * kr], axis=0).astype(BF16)
        z = jnp.dot(m2_ref[b], y, preferred_element_type=F32).astype(o_ref.dtype)
        o_ref[0, b] = z[:n2]
        o_ref[1, b] = z[n2:]


def _middle(m1, m2, a4, kf, order):
    _, k1p, n2, C = a4.shape
    kb = K1_PER_STEP
    mat = pl.BlockSpec((kb, 2 * n2, 2 * n2), lambda a: (a, 0, 0))
    return pl.pallas_call(
        _middle_kernel,
        out_shape=jax.ShapeDtypeStruct((2, k1p, n2, C), BF16),
        grid=(k1p // kb,),
        in_specs=[mat, mat,
                  pl.BlockSpec((2, kb, n2, C), lambda a: (0, a, 0, 0)),
                  pl.BlockSpec((kb, 2 * n2, C), lambda a: (a, 0, order))],
        out_specs=pl.BlockSpec((2, kb, n2, C), lambda a: (0, a, 0, 0)),
        compiler_params=_cparams(("parallel",)),
        name="dft_middle",
    )(m1, m2, a4, kf)


def _hyena_long_convs(v, g1, kfilt2d, L):
    C = v.shape[1]
    cs = _dft_constants(L)
    nh, k1p = cs["nh"], cs["k1p"]
    fa = _lmat(cs["fa_filt"], kfilt2d)
    kspec = _filter_spectrum(cs["m1"], fa.reshape(2, k1p, FFT_MINOR, 2 * C))
    a = _lmat(cs["fa_data"], v.reshape(nh, FFT_MINOR * C))
    z = _middle(cs["m1"], cs["m2"], a.reshape(2, k1p, FFT_MINOR, C), kspec, 0)
    y2d = _lmat_gate(cs["g"], z.reshape(2 * k1p, FFT_MINOR * C), g1.reshape(nh, FFT_MINOR * C))
    a = _lmat(cs["fa_data"], y2d)
    z = _middle(cs["m1"], cs["m2"], a.reshape(2, k1p, FFT_MINOR, C), kspec, 1)
    return _lmat(cs["g"], z.reshape(2 * k1p, FFT_MINOR * C), F32).reshape(L, C)


def _out_kernel(yh_ref, gate_ref, ya_ref, wh_ref, wa_ref, ng_ref, x_ref, g_ref, b_ref, o_ref):
    yh = yh_ref[...] * gate_ref[...]
    yh = yh * lax.rsqrt(jnp.mean(yh * yh, axis=-1, keepdims=True) + LN_EPS) * ng_ref[...]
    acc = jnp.dot(yh.astype(BF16), wh_ref[...], preferred_element_type=F32)
    acc = acc + jnp.dot(ya_ref[...], wa_ref[...], preferred_element_type=F32)
    o_ref[...] = _layer_norm(ALPHA * x_ref[...] + acc, g_ref[...], b_ref[...])


def _out_ln(y_hy, z, gate_block, y_at, w_out, norm_g, x1, g, b, tm=512):
    L, D = x1.shape
    ch, ca = y_hy.shape[1], y_at.shape[1]
    tm = min(tm, L)
    row = lambda c: pl.BlockSpec((tm, c), lambda i: (i, 0))
    full = lambda r, c: pl.BlockSpec((r, c), lambda i: (0, 0))
    return pl.pallas_call(
        _out_kernel,
        out_shape=jax.ShapeDtypeStruct((L, D), F32),
        grid=(L // tm,),
        in_specs=[row(ch), pl.BlockSpec((tm, ch), lambda i: (i, gate_block)), row(ca),
                  pl.BlockSpec((ch, D), lambda i: (0, 0)),
                  pl.BlockSpec((ca, D), lambda i: (1, 0)),
                  full(1, ch), row(D), full(1, D), full(1, D)],
        out_specs=row(D),
        compiler_params=_cparams(("parallel",)),
        name="out_ln",
    )(y_hy, z, y_at, w_out, w_out, norm_g, x1, g, b)


def kernel(x, ffn1_w_gate, ffn1_w_up, ffn1_w_down, ln1_g, ln1_b, w_in, hyena_conv_w, hyena_conv_b, filt_w1, filt_b1, filt_w2, filt_b2, filt_w3, filt_b3, filt_freq, filt_w_out, hyena_bias, hyena_norm_g, lambda_q1, lambda_k1, lambda_q2, lambda_k2, subln_g, w_out, ln2_g, ln2_b, ffn2_w_gate, ffn2_w_up, ffn2_w_down, ln3_g, ln3_b):
    assert x.shape[0] == 1 and ffn1_w_gate.shape[0] == DEPTH
    L = x.shape[1]
    bf = lambda a: a.astype(BF16)
    row = lambda a: a.reshape(1, -1)
    h = x[0]
    s_hy = 3 * D_HYENA
    for i in range(DEPTH):
        h = _ffn_ln(h, bf(ffn1_w_gate[i]), bf(ffn1_w_up[i]), bf(ffn1_w_down[i]), row(ln1_g[i]), row(ln1_b[i]))
        w_i = w_in[i]
        p_hy = _proj(h, bf(w_i[:, :s_hy]), BF16)
        q_t = _proj_q_t(h, bf(w_i[:, s_hy:s_hy + D_QK]))
        k_at = _proj_rope(h, bf(w_i[:, s_hy + D_QK:s_hy + 2 * D_QK]))
        v_t = _proj_v_t(h, bf(w_i[:, s_hy + 2 * D_QK:]), ATTN_TK)
        z = _short_conv(p_hy, hyena_conv_w[i], row(hyena_conv_b[i]))
        kfilt = _hyena_filters(L, filt_w1[i], filt_b1[i], filt_w2[i], filt_b2[i], filt_w3[i], filt_b3[i],
                               filt_freq[i], filt_w_out[i], hyena_bias[i])
        y_hy = _hyena_long_convs(z[:, :D_HYENA], z[:, D_HYENA:2 * D_HYENA], kfilt, L)
        y_at = _diff_attention(q_t, k_at, v_t, row(lambda_q1[i]), row(lambda_k1[i]), row(lambda_q2[i]),
                               row(lambda_k2[i]), row(subln_g[i]))
        h = _out_ln(y_hy, z, 2, y_at, bf(w_out[i]), row(hyena_norm_g[i]), h, row(ln2_g[i]), row(ln2_b[i]))
        h = _ffn_ln(h, bf(ffn2_w_gate[i]), bf(ffn2_w_up[i]), bf(ffn2_w_down[i]), row(ln3_g[i]), row(ln3_b[i]))
    return h[None]
```

```python
import functools
import math

import numpy as np
import jax
import jax.numpy as jnp
from jax import lax
from jax.experimental import pallas as pl
from jax.experimental.pallas import tpu as pltpu

F32 = jnp.float32
BF16 = jnp.bfloat16

D_HYENA = 1024
N_HEADS = 8
HEAD_DIM = 64
VAL_DIM = 128
D_QK = 2 * N_HEADS * HEAD_DIM
ROT_DIM = 16
ROPE_THETA = 500000.0
FILTER_EMB = 33
FILTER_HIDDEN = 64
DECAY_TARGET = 1e-2
MIN_DECAY = math.log(DECAY_TARGET) / 1.5
MAX_DECAY = math.log(DECAY_TARGET) / 0.3
LN_EPS = 1e-5
DEPTH = 1
ALPHA = (2.0 * DEPTH) ** 0.25
LAMBDA_INIT = 0.8 - 0.6 * math.exp(-0.3 * 0)
LOG2E = math.log2(math.e)

LANES = 128
BF16_SUBLANES = 16
FFT_MINOR = 128
VMEM_LIMIT = 56 * 1024 * 1024
ATTN_TQ = 1024
ATTN_TK = 1024
FIXED_SHIFT_MIN_DENOM = 2.0 ** -60


def _cparams(sem):
    return pltpu.CompilerParams(dimension_semantics=sem, vmem_limit_bytes=VMEM_LIMIT)


def _layer_norm(y, g, b):
    mu = jnp.mean(y, axis=-1, keepdims=True)
    d = y - mu
    var = jnp.mean(d * d, axis=-1, keepdims=True)
    return d * lax.rsqrt(var + LN_EPS) * g + b


def _ffn_kernel(x_ref, wg_ref, wu_ref, wd_ref, g_ref, b_ref, o_ref, xb_ref, acc_ref, h_ref, *, nf):
    f = pl.program_id(1)

    def up(slot):
        xb = xb_ref[...]
        hg = jnp.dot(xb, wg_ref[...], preferred_element_type=F32)
        hu = jnp.dot(xb, wu_ref[...], preferred_element_type=F32)
        h_ref[slot] = (hg * jax.nn.sigmoid(hg) * hu).astype(BF16)

    def down(slot):
        acc_ref[...] += jnp.dot(h_ref[slot], wd_ref[...], preferred_element_type=F32)

    @pl.when(f == 0)
    def _():
        xb_ref[...] = x_ref[...].astype(BF16)
        acc_ref[...] = jnp.zeros_like(acc_ref)
        up(0)

    @pl.when(jnp.logical_and(f > 0, f < nf))
    def _():
        down((f - 1) % 2)
        up(f % 2)

    @pl.when(f == nf)
    def _():
        down((nf - 1) % 2)
        y = ALPHA * x_ref[...] + 0.5 * acc_ref[...]
        o_ref[...] = _layer_norm(y, g_ref[...], b_ref[...])


def _ffn_ln(x, wg, wu, wd, g, b, tm=1024, tf=512):
    L, D = x.shape
    nf = wg.shape[1] // tf
    tm = min(tm, L)
    up_w = pl.BlockSpec((D, tf), lambda i, f: (0, jnp.minimum(f, nf - 1)))
    once = dict(pipeline_mode=pl.Buffered(1))
    return pl.pallas_call(
        functools.partial(_ffn_kernel, nf=nf),
        out_shape=jax.ShapeDtypeStruct((L, D), F32),
        grid=(L // tm, nf + 1),
        in_specs=[
            pl.BlockSpec((tm, D), lambda i, f: (i, 0), **once),
            up_w, up_w,
            pl.BlockSpec((tf, D), lambda i, f: (jnp.maximum(f - 1, 0), 0)),
            pl.BlockSpec((1, D), lambda i, f: (0, 0)),
            pl.BlockSpec((1, D), lambda i, f: (0, 0)),
        ],
        out_specs=pl.BlockSpec((tm, D), lambda i, f: (i, 0), **once),
        scratch_shapes=[pltpu.VMEM((tm, D), BF16), pltpu.VMEM((tm, D), F32), pltpu.VMEM((2, tm, tf), BF16)],
        compiler_params=_cparams(("parallel", "arbitrary")),
        name="ffn_ln",
    )(x, wg, wu, wd, g, b)


def _proj_kernel(x_ref, w_ref, o_ref, xb_ref):
    @pl.when(pl.program_id(1) == 0)
    def _():
        xb_ref[...] = x_ref[...].astype(BF16)

    o_ref[...] = jnp.dot(xb_ref[...], w_ref[...], preferred_element_type=F32).astype(o_ref.dtype)


def _proj(x, w, out_dtype, tm=1024, tn=1024):
    L, D = x.shape
    N = w.shape[1]
    tm = min(tm, L)
    return pl.pallas_call(
        _proj_kernel,
        out_shape=jax.ShapeDtypeStruct((L, N), out_dtype),
        grid=(L // tm, N // tn),
        in_specs=[pl.BlockSpec((tm, D), lambda i, j: (i, 0)),
                  pl.BlockSpec((D, tn), lambda i, j: (0, j))],
        out_specs=pl.BlockSpec((tm, tn), lambda i, j: (i, j)),
        scratch_shapes=[pltpu.VMEM((tm, D), BF16)],
        compiler_params=_cparams(("parallel", "arbitrary")),
        name="proj",
    )(x, w)


def _rope_tables(L, scale):
    half = ROT_DIM // 2
    inv = ROPE_THETA ** (-np.arange(0, ROT_DIM, 2, dtype=np.float64) / ROT_DIM)
    pos = np.arange(L, dtype=np.float64)[:, None]
    d = np.arange(LANES) % HEAD_DIM
    ang = pos * inv[d % half][None, :]
    c = np.where(d[None, :] < ROT_DIM, np.cos(ang), 1.0) * scale
    s1 = np.where(d[None, :] < half, -np.sin(ang), 0.0) * scale
    s2 = np.where((d[None, :] >= half) & (d[None, :] < ROT_DIM), np.sin(ang), 0.0) * scale
    return c, s1, s2


def _proj_rope_kernel(x_ref, w_ref, c_ref, s1_ref, s2_ref, o_ref, xb_ref, *, tn):
    @pl.when(pl.program_id(1) == 0)
    def _():
        xb_ref[...] = x_ref[...].astype(BF16)

    p = jnp.dot(xb_ref[...], w_ref[...], preferred_element_type=F32)
    c, s1, s2 = c_ref[...], s1_ref[...], s2_ref[...]
    shift = ROT_DIM // 2
    for h in range(tn // LANES):
        ph = p[:, h * LANES:(h + 1) * LANES]
        r = ph * c + pltpu.roll(ph, LANES - shift, 1) * s1 + pltpu.roll(ph, shift, 1) * s2
        o_ref[:, h * LANES:(h + 1) * LANES] = r.astype(o_ref.dtype)


def _proj_rope(x, w, tm=1024, tn=1024):
    L, D = x.shape
    N = w.shape[1]
    tm = min(tm, L)
    tabs = [jnp.asarray(t, F32) for t in _rope_tables(L, 1.0)]
    tab = pl.BlockSpec((tm, LANES), lambda i, j: (i, 0))
    return pl.pallas_call(
        functools.partial(_proj_rope_kernel, tn=tn),
        out_shape=jax.ShapeDtypeStruct((L, N), BF16),
        grid=(L // tm, N // tn),
        in_specs=[pl.BlockSpec((tm, D), lambda i, j: (i, 0)),
                  pl.BlockSpec((D, tn), lambda i, j: (0, j)),
                  tab, tab, tab],
        out_specs=pl.BlockSpec((tm, tn), lambda i, j: (i, j)),
        scratch_shapes=[pltpu.VMEM((tm, D), BF16)],
        compiler_params=_cparams(("parallel", "arbitrary")),
        name="proj_rope",
    )(x, w, *tabs)


def _projT_kernel(x_ref, w_ref, *rest, rope, tn, tk):
    if rope:
        c_ref, s1_ref, s2_ref, o_ref, xb_ref = rest
    else:
        o_ref, xb_ref = rest

    @pl.when(pl.program_id(1) == 0)
    def _():
        xb_ref[...] = x_ref[...].astype(BF16)

    pt = jnp.dot(xb_ref[...], w_ref[...], preferred_element_type=F32).T
    if rope:
        c, s1, s2 = c_ref[...], s1_ref[...], s2_ref[...]
        shift = ROT_DIM // 2
        for h in range(tn // LANES):
            ph = pt[h * LANES:(h + 1) * LANES, :]
            r = ph * c + pltpu.roll(ph, LANES - shift, 0) * s1 + pltpu.roll(ph, shift, 0) * s2
            o_ref[h * LANES:(h + 1) * LANES, :] = r.astype(o_ref.dtype)
    else:
        for s in range(pt.shape[1] // tk):
            o_ref[s] = pt[:, s * tk:(s + 1) * tk].astype(o_ref.dtype)


def _proj_q_t(x, w, tm=1024, tn=512):
    L, D = x.shape
    N = w.shape[1]
    tm = min(tm, L)
    tabs = [jnp.asarray(t.T, F32) for t in _rope_tables(L, HEAD_DIM ** -0.5 * LOG2E)]
    tab = pl.BlockSpec((LANES, tm), lambda i, j: (0, i))
    return pl.pallas_call(
        functools.partial(_projT_kernel, rope=True, tn=tn, tk=None),
        out_shape=jax.ShapeDtypeStruct((N, L), BF16),
        grid=(L // tm, N // tn),
        in_specs=[pl.BlockSpec((tm, D), lambda i, j: (i, 0)),
                  pl.BlockSpec((D, tn), lambda i, j: (0, j)),
                  tab, tab, tab],
        out_specs=pl.BlockSpec((tn, tm), lambda i, j: (j, i)),
        scratch_shapes=[pltpu.VMEM((tm, D), BF16)],
        compiler_params=_cparams(("parallel", "arbitrary")),
        name="proj_q_t",
    )(x, w, *tabs)


def _proj_v_t(x, w, tk, tm=1024, tn=512):
    L, D = x.shape
    N = w.shape[1]
    tm = min(tm, L)
    tk = min(tk, L)
    per = tm // tk
    return pl.pallas_call(
        functools.partial(_projT_kernel, rope=False, tn=tn, tk=tk),
        out_shape=jax.ShapeDtypeStruct((L // tk, N, tk), BF16),
        grid=(L // tm, N // tn),
        in_specs=[pl.BlockSpec((tm, D), lambda i, j: (i, 0)),
                  pl.BlockSpec((D, tn), lambda i, j: (0, j))],
        out_specs=pl.BlockSpec((per, tn, tk), lambda i, j: (i, j, 0)),
        scratch_shapes=[pltpu.VMEM((tm, D), BF16)],
        compiler_params=_cparams(("parallel", "arbitrary")),
        name="proj_v_t",
    )(x, w)


def _attn_kernel(qt_ref, k_ref, vt_ref, lq1_ref, lk1_ref, lq2_ref, lk2_ref, g_ref, o_ref,
                 acc_ref, l_ref, kn_ref, sa_ref, sb_ref, pa_ref, pb_ref, *, tk, nk):
    qt = qt_ref[...].astype(F32)
    tq = qt.shape[1]
    row = lax.broadcasted_iota(jnp.int32, qt.shape, 0)
    qs = (jnp.where(row < HEAD_DIM, qt, 0.0).astype(BF16),
          jnp.where(row >= HEAD_DIM, qt, 0.0).astype(BF16))
    zero = jnp.zeros((1, tq), F32)

    def k_chunk(j):
        return k_ref[pl.ds(pl.multiple_of(j * tk, tk), tk), :]

    @pl.when(pl.program_id(1) == 0)
    def _():
        def kbody(c, carry):
            kb = k_chunk(c).astype(F32)
            sq = kb * kb
            lane = lax.broadcasted_iota(jnp.int32, sq.shape, 1)
            n0 = jnp.sum(jnp.where(lane < HEAD_DIM, sq, 0.0), axis=1, keepdims=True)
            n1 = jnp.sum(jnp.where(lane >= HEAD_DIM, sq, 0.0), axis=1, keepdims=True)
            return (jnp.maximum(carry[0], jnp.max(n0, axis=0, keepdims=True)),
                    jnp.maximum(carry[1], jnp.max(n1, axis=0, keepdims=True)))

        z = jnp.zeros((1, 1), F32)
        k0, k1 = lax.fori_loop(0, nk, kbody, (z, z))
        kn_ref[0] = jnp.broadcast_to(k0, (1, LANES))
        kn_ref[1] = jnp.broadcast_to(k1, (1, LANES))

    q2 = qt * qt
    qn = (jnp.sum(q2[:HEAD_DIM], axis=0, keepdims=True), jnp.sum(q2[HEAD_DIM:], axis=0, keepdims=True))
    mb = tuple(jnp.sqrt(qn[mi] * kn_ref[mi][:, 0:1]) for mi in range(2))
    acc_ref[...] = jnp.zeros_like(acc_ref)

    def fixed_probs(j, p_ref):
        kb = k_chunk(j)
        sums = []
        for mi in range(2):
            pt = jnp.exp2(jnp.dot(kb, qs[mi], preferred_element_type=F32) - mb[mi])
            p_ref[mi] = pt.astype(BF16)
            sums.append(jnp.sum(pt, axis=0, keepdims=True))
        return tuple(sums)

    def fixed_accumulate(j, p_ref):
        vb = vt_ref[j]
        for mi in range(2):
            acc_ref[mi] += jnp.dot(vb, p_ref[mi], preferred_element_type=F32)

    def fixed_body(jj, carry):
        ls, la = carry
        j = 2 * jj
        lb = fixed_probs(j + 1, pb_ref)
        fixed_accumulate(j, pa_ref)
        ls = (ls[0] + la[0] + lb[0], ls[1] + la[1] + lb[1])
        la = fixed_probs(j + 2, pa_ref)
        fixed_accumulate(j + 1, pb_ref)
        return ls, la

    ls, la = lax.fori_loop(0, nk // 2 - 1, fixed_body, ((zero, zero), fixed_probs(0, pa_ref)))
    lb = fixed_probs(nk - 1, pb_ref)
    fixed_accumulate(nk - 2, pa_ref)
    fixed_accumulate(nk - 1, pb_ref)
    ls = (ls[0] + la[0] + lb[0], ls[1] + la[1] + lb[1])
    l_ref[0] = ls[0]
    l_ref[1] = ls[1]
    trusted = jnp.min(jnp.minimum(ls[0], ls[1])) >= FIXED_SHIFT_MIN_DENOM

    @pl.when(jnp.logical_not(trusted))
    def _():
        acc_ref[...] = jnp.zeros_like(acc_ref)
        neg = jnp.full((1, tq), -jnp.inf, F32)

        def scores(j, s_ref):
            kb = k_chunk(j)
            mx = []
            for mi in range(2):
                st = jnp.dot(kb, qs[mi], preferred_element_type=F32)
                s_ref[mi] = st
                mx.append(jnp.max(st, axis=0, keepdims=True))
            return tuple(mx)

        def consume(j, s_ref, mx, ms, lr):
            vb = vt_ref[j]
            new_m, new_l = [], []
            for mi in range(2):
                m_next = jnp.maximum(ms[mi], mx[mi])
                alpha = jnp.exp2(ms[mi] - m_next)
                pt = jnp.exp2(s_ref[mi] - m_next)
                new_l.append(alpha * lr[mi] + jnp.sum(pt, axis=0, keepdims=True))
                new_m.append(m_next)
                acc_ref[mi] = acc_ref[mi] * alpha + jnp.dot(vb, pt.astype(BF16), preferred_element_type=F32)
            return tuple(new_m), tuple(new_l)

        def body(jj, carry):
            ms, lr, mxa = carry
            j = 2 * jj
            mxb = scores(j + 1, sb_ref)
            ms, lr = consume(j, sa_ref, mxa, ms, lr)
            mxa = scores(jnp.minimum(j + 2, nk - 1), sa_ref)
            ms, lr = consume(j + 1, sb_ref, mxb, ms, lr)
            return ms, lr, mxa

        _, lr, _ = lax.fori_loop(0, nk // 2, body, ((neg, neg), (zero, zero), scores(0, sa_ref)))
        l_ref[0] = lr[0]
        l_ref[1] = lr[1]

    lam = (jnp.exp(jnp.sum(lq1_ref[...] * lk1_ref[...], axis=-1, keepdims=True))
           - jnp.exp(jnp.sum(lq2_ref[...] * lk2_ref[...], axis=-1, keepdims=True)) + LAMBDA_INIT)
    ot = acc_ref[0] / l_ref[0] - lam * (acc_ref[1] / l_ref[1])
    ot = ot * lax.rsqrt(jnp.mean(ot * ot, axis=0, keepdims=True) + LN_EPS)
    o_ref[...] = (ot.T * (g_ref[...] * (1.0 - LAMBDA_INIT))).astype(o_ref.dtype)


def _diff_attention(qt, k, vt, lq1, lk1, lq2, lk2, subln_g, tq=ATTN_TQ):
    L = k.shape[0]
    nk, _, tk = vt.shape
    assert nk % 2 == 0, "the kv loop handles chunks in pairs"
    tq = min(tq, L)
    vec = pl.BlockSpec((1, HEAD_DIM), lambda h, i: (0, 0))
    return pl.pallas_call(
        functools.partial(_attn_kernel, tk=tk, nk=nk),
        out_shape=jax.ShapeDtypeStruct((L, N_HEADS * VAL_DIM), BF16),
        grid=(N_HEADS, L // tq),
        in_specs=[pl.BlockSpec((LANES, tq), lambda h, i: (h, i)),
                  pl.BlockSpec((L, LANES), lambda h, i: (0, h)),
                  pl.BlockSpec((nk, VAL_DIM, tk), lambda h, i: (0, h, 0)),
                  vec, vec, vec, vec,
                  pl.BlockSpec((1, VAL_DIM), lambda h, i: (0, 0))],
        out_specs=pl.BlockSpec((tq, VAL_DIM), lambda h, i: (i, h)),
        scratch_shapes=[pltpu.VMEM((2, VAL_DIM, tq), F32), pltpu.VMEM((2, 1, tq), F32),
                        pltpu.VMEM((2, 1, LANES), F32),
                        pltpu.VMEM((2, tk, tq), F32), pltpu.VMEM((2, tk, tq), F32),
                        pltpu.VMEM((2, tk, tq), BF16), pltpu.VMEM((2, tk, tq), BF16)],
        compiler_params=_cparams(("parallel", "arbitrary")),
        name="diff_attention",
    )(qt, k, vt, lq1, lk1, lq2, lk2, subln_g)


def _conv_kernel(x_ref, w_ref, b_ref, o_ref, *, rows, nchunks):
    w0 = w_ref[0:1, :]
    w1 = w_ref[1:2, :]
    w2 = w_ref[2:3, :]
    b = b_ref[...]
    grp = BF16_SUBLANES
    last = nchunks * rows - grp

    def body(i, carry):
        r0 = pl.multiple_of(i * rows, rows)
        x0 = x_ref[pl.ds(r0, rows), :].astype(F32)
        pr = x_ref[pl.ds(pl.multiple_of(jnp.maximum(r0 - grp, 0), grp), grp), :].astype(F32)
        nx = x_ref[pl.ds(pl.multiple_of(jnp.minimum(r0 + rows, last), grp), grp), :].astype(F32)
        prev_row = jnp.where(i > 0, pr[grp - 1:grp, :], 0.0)
        next_row = jnp.where(i < nchunks - 1, nx[0:1, :], 0.0)
        row = lax.broadcasted_iota(jnp.int32, x0.shape, 0)
        xm = jnp.where(row == 0, prev_row, pltpu.roll(x0, 1, 0))
        xp = jnp.where(row == rows - 1, next_row, pltpu.roll(x0, rows - 1, 0))
        o_ref[pl.ds(r0, rows), :] = (xm * w0 + x0 * w1 + xp * w2 + b).astype(o_ref.dtype)
        return carry

    lax.fori_loop(0, nchunks, body, 0)


def _short_conv(p, w, b, tn=256, rows=512):
    L, C = p.shape
    rows = min(rows, L)
    return pl.pallas_call(
        functools.partial(_conv_kernel, rows=rows, nchunks=L // rows),
        out_shape=jax.ShapeDtypeStruct((L, C), BF16),
        grid=(C // tn,),
        in_specs=[pl.BlockSpec((L, tn), lambda j: (0, j)),
                  pl.BlockSpec((3, tn), lambda j: (0, j)),
                  pl.BlockSpec((1, tn), lambda j: (0, j))],
        out_specs=pl.BlockSpec((L, tn), lambda j: (0, j)),
        compiler_params=_cparams(("parallel",)),
        name="short_conv",
    )(p, w, b)


def _filter_kernel(z_ref, w1_ref, b1_ref, w2_ref, b2_ref, w3_ref, b3_ref, fr_ref, wf_ref, wb_ref,
                   dl_ref, bias_ref, o_ref, *, L, nh):
    hp = lax.Precision.HIGHEST
    fr = fr_ref[...]
    h = jnp.sin(fr * (jnp.dot(z_ref[...], w1_ref[...], precision=hp, preferred_element_type=F32) + b1_ref[...]))
    h = jnp.sin(fr * (jnp.dot(h, w2_ref[...], precision=hp, preferred_element_type=F32) + b2_ref[...]))
    h = jnp.sin(fr * (jnp.dot(h, w3_ref[...], precision=hp, preferred_element_type=F32) + b3_ref[...]))
    hb = h.astype(BF16)

    def out_proj(w_ref):
        return jnp.dot(hb, w_ref[...], preferred_element_type=F32)

    rb = z_ref.shape[0]
    nq = rb // nh
    c2 = dl_ref.shape[1]
    row = lax.broadcasted_iota(jnp.int32, (rb, 1), 0)
    assert nh & (nh - 1) == 0
    r = jnp.bitwise_and(row, nh - 1)
    q_idx = lax.shift_right_logical(row, nh.bit_length() - 1)
    n = FFT_MINOR * r + nq * pl.program_id(0) + q_idx
    step = 1.0 / (L - 1)
    dl = dl_ref[...]
    kf = out_proj(wf_ref) * jnp.exp(-(n.astype(F32) * step) * dl)
    kf = (kf + jnp.where(n == 0, bias_ref[...], 0.0)).astype(o_ref.dtype)
    kb = out_proj(wb_ref) * jnp.exp(-((L - n).astype(F32) * step) * dl)
    kb = jnp.where(n == 0, 0.0, kb).astype(o_ref.dtype)
    for q in range(nq):
        o_ref[0:nh, q * c2:(q + 1) * c2] = kf[q * nh:(q + 1) * nh]
        o_ref[nh:2 * nh, q * c2:(q + 1) * c2] = kb[q * nh:(q + 1) * nh]


def _filter_features(L, nq):
    half = LANES // 2
    nh = L // FFT_MINOR
    bands = (FILTER_EMB - 1) // 2
    f = np.linspace(1e-4, bands - 1, bands)
    z = np.zeros((L, LANES))
    step, q, r = np.meshgrid(np.arange(FFT_MINOR // nq), np.arange(nq), np.arange(nh), indexing="ij")
    n = (FFT_MINOR * r + nq * step + q).reshape(-1).astype(np.float64)
    for off, pos in ((0, n), (half, L - n)):
        w = 2.0 * np.pi * pos / L
        z[:, off] = pos / (L - 1)
        z[:, off + 1:off + 1 + bands] = np.cos(f[None, :] * w[:, None])
        z[:, off + 1 + bands:off + 1 + 2 * bands] = -np.sin(f[None, :] * w[:, None])
    return jnp.asarray(z, F32)


def _pad_to(a, shape):
    return jnp.pad(a, [(0, s - d) for d, s in zip(a.shape, shape)])


def _block_diag2(a):
    half = LANES // 2
    p = _pad_to(a, (half, half))
    z = jnp.zeros_like(p)
    return jnp.concatenate([jnp.concatenate([p, z], axis=1), jnp.concatenate([z, p], axis=1)], axis=0)


def _hyena_filters(L, fw1, fb1, fw2, fb2, fw3, fb3, ffreq, fw_out, bias, nq=8):
    H = LANES
    half = H // 2
    C2 = 2 * D_HYENA
    nh = L // FFT_MINOR
    rb = nq * nh
    dup = lambda v: jnp.tile(_pad_to(v[None, :], (1, half)), (1, 2))
    wo = fw_out.reshape(FILTER_HIDDEN, 2, 2, D_HYENA)
    wf = _pad_to(wo[:, :, 0, :].reshape(FILTER_HIDDEN, C2), (H, C2))
    wb = jnp.concatenate([jnp.zeros((half, C2), F32), _pad_to(wo[:, :, 1, :].reshape(FILTER_HIDDEN, C2), (half, C2))])
    deltas = np.abs(np.linspace(MIN_DECAY, MAX_DECAY, D_HYENA))
    dl = jnp.asarray(np.tile(deltas, 2)[None, :], F32)
    full = lambda shape: pl.BlockSpec(shape, lambda i: (0,) * len(shape))
    return pl.pallas_call(
        functools.partial(_filter_kernel, L=L, nh=nh),
        out_shape=jax.ShapeDtypeStruct((2 * nh, FFT_MINOR * C2), BF16),
        grid=(L // rb,),
        in_specs=[pl.BlockSpec((rb, H), lambda i: (i, 0)),
                  full((H, H)), full((1, H)), full((H, H)), full((1, H)), full((H, H)), full((1, H)),
                  full((1, H)),
                  full((H, C2)), full((H, C2)),
                  full((1, C2)), full((1, C2))],
        out_specs=pl.BlockSpec((2 * nh, nq * C2), lambda i: (0, i)),
        compiler_params=_cparams(("parallel",)),
        name="hyena_filters",
    )(_filter_features(L, nq), _block_diag2(fw1), dup(fb1), _block_diag2(fw2), dup(fb2), _block_diag2(fw3),
      dup(fb3), dup(ffreq), wf.astype(BF16), wb.astype(BF16), dl, bias.reshape(1, C2))


def _dft_constants(L):
    N = 2 * L
    N1 = N // FFT_MINOR
    nh = N1 // 2
    k1h = N1 // 2 + 1
    k1p = -(-k1h // 8) * 8
    k1 = np.arange(k1h)[:, None]

    def stage_a(nn):
        th = 2 * np.pi * np.arange(nn)[None, :] * k1 / N1
        fa = np.zeros((2 * k1p, nn))
        fa[:k1h] = np.cos(th)
        fa[k1p:k1p + k1h] = -np.sin(th)
        return fa

    n2 = np.arange(FFT_MINOR)
    m1 = np.zeros((k1p, 2 * FFT_MINOR, 2 * FFT_MINOR))
    m2 = np.zeros_like(m1)
    for a in range(k1h):
        ph = -2 * np.pi * (n2[None, :] * a / N + n2[None, :] * n2[:, None] / FFT_MINOR)
        cr, ci = np.cos(ph), np.sin(ph)
        m1[a] = np.block([[cr, -ci], [ci, cr]])
        m2[a] = np.block([[cr.T, ci.T], [-ci.T, cr.T]])
    kk = np.arange(k1h)[None, :]
    cw = np.where((kk == 0) | (kk == N1 // 2), 1.0, 2.0) / N
    th = 2 * np.pi * np.arange(nh)[:, None] * kk / N1
    g = np.zeros((nh, 2 * k1p))
    g[:, :k1h] = cw * np.cos(th)
    g[:, k1p:k1p + k1h] = -cw * np.sin(th)
    as_bf16 = lambda a: jnp.asarray(a, F32).astype(BF16)
    return dict(nh=nh, n1=N1, k1h=k1h, k1p=k1p, fa_data=as_bf16(stage_a(nh)), fa_filt=as_bf16(stage_a(N1)),
                m1=as_bf16(m1), m2=as_bf16(m2), g=as_bf16(g))


def _lmat_kernel(f_ref, x_ref, o_ref):
    o_ref[...] = jnp.dot(f_ref[...], x_ref[...].astype(BF16), preferred_element_type=F32).astype(o_ref.dtype)


def _lmat(fmat, x2d, out_dtype=BF16, tn=16384):
    M, K = fmat.shape
    ncols = x2d.shape[1]
    tn = min(tn, ncols)
    return pl.pallas_call(
        _lmat_kernel,
        out_shape=jax.ShapeDtypeStruct((M, ncols), out_dtype),
        grid=(ncols // tn,),
        in_specs=[pl.BlockSpec((M, K), lambda j: (0, 0)),
                  pl.BlockSpec((K, tn), lambda j: (0, j))],
        out_specs=pl.BlockSpec((M, tn), lambda j: (0, j)),
        compiler_params=_cparams(("parallel",)),
        name="dft_stage_a",
    )(fmat, x2d)


def _lmat_gate_kernel(g_ref, z_ref, gate_ref, o_ref):
    o_ref[...] = jnp.dot(g_ref[...], z_ref[...], preferred_element_type=F32) * gate_ref[...]


def _lmat_gate(gmat, z2d, gate2d, tn=16384):
    M, K = gmat.shape
    ncols = z2d.shape[1]
    tn = min(tn, ncols)
    return pl.pallas_call(
        _lmat_gate_kernel,
        out_shape=jax.ShapeDtypeStruct((M, ncols), F32),
        grid=(ncols // tn,),
        in_specs=[pl.BlockSpec((M, K), lambda j: (0, 0)),
                  pl.BlockSpec((K, tn), lambda j: (0, j)),
                  pl.BlockSpec((M, tn), lambda j: (0, j))],
        out_specs=pl.BlockSpec((M, tn), lambda j: (0, j)),
        compiler_params=_cparams(("parallel",)),
        name="dft_stage_a_inv",
    )(gmat, z2d, gate2d)


K1_PER_STEP = 8


def _filter_spectrum_kernel(m1_ref, a_ref, o_ref):
    for b in range(m1_ref.shape[0]):
        x = jnp.concatenate([a_ref[0, b], a_ref[1, b]], axis=0)
        o_ref[b] = jnp.dot(m1_ref[b], x, preferred_element_type=F32).astype(o_ref.dtype)


def _filter_spectrum(m1, a4):
    _, k1p, n2, C = a4.shape
    kb = K1_PER_STEP
    return pl.pallas_call(
        _filter_spectrum_kernel,
        out_shape=jax.ShapeDtypeStruct((k1p, 2 * n2, C), BF16),
        grid=(k1p // kb,),
        in_specs=[pl.BlockSpec((kb, 2 * n2, 2 * n2), lambda a: (a, 0, 0)),
                  pl.BlockSpec((2, kb, n2, C), lambda a: (0, a, 0, 0))],
        out_specs=pl.BlockSpec((kb, 2 * n2, C), lambda a: (a, 0, 0)),
        compiler_params=_cparams(("parallel",)),
        name="filter_spectrum",
    )(m1, a4)


def _middle_kernel(m1_ref, m2_ref, a_ref, kf_ref, o_ref):
    n2 = a_ref.shape[2]
    for b in range(m1_ref.shape[0]):
        x = jnp.concatenate([a_ref[0, b], a_ref[1, b]], axis=0)
        s = jnp.dot(m1_ref[b], x, preferred_element_type=F32)
        sr, si = s[:n2], s[n2:]
        kf = kf_ref[b].astype(F32)
        kr, ki = kf[:n2], kf[n2:]
        y = jnp.concatenate([sr * kr - si * ki, sr * ki + si * kr], axis=0).astype(BF16)
        z = jnp.dot(m2_ref[b], y, preferred_element_type=F32).astype(o_ref.dtype)
        o_ref[0, b] = z[:n2]
        o_ref[1, b] = z[n2:]


def _middle(m1, m2, a4, kf, order):
    _, k1p, n2, C = a4.shape
    kb = K1_PER_STEP
    mat = pl.BlockSpec((kb, 2 * n2, 2 * n2), lambda a: (a, 0, 0))
    return pl.pallas_call(
        _middle_kernel,
        out_shape=jax.ShapeDtypeStruct((2, k1p, n2, C), BF16),
        grid=(k1p // kb,),
        in_specs=[mat, mat,
                  pl.BlockSpec((2, kb, n2, C), lambda a: (0, a, 0, 0)),
                  pl.BlockSpec((kb, 2 * n2, C), lambda a: (a, 0, order))],
        out_specs=pl.BlockSpec((2, kb, n2, C), lambda a: (0, a, 0, 0)),
        compiler_params=_cparams(("parallel",)),
        name="dft_middle",
    )(m1, m2, a4, kf)


def _hyena_long_convs(v, g1, kfilt2d, L):
    C = v.shape[1]
    cs = _dft_constants(L)
    nh, k1p = cs["nh"], cs["k1p"]
    fa = _lmat(cs["fa_filt"], kfilt2d)
    kspec = _filter_spectrum(cs["m1"], fa.reshape(2, k1p, FFT_MINOR, 2 * C))
    a = _lmat(cs["fa_data"], v.reshape(nh, FFT_MINOR * C))
    z = _middle(cs["m1"], cs["m2"], a.reshape(2, k1p, FFT_MINOR, C), kspec, 0)
    y2d = _lmat_gate(cs["g"], z.reshape(2 * k1p, FFT_MINOR * C), g1.reshape(nh, FFT_MINOR * C))
    a = _lmat(cs["fa_data"], y2d)
    z = _middle(cs["m1"], cs["m2"], a.reshape(2, k1p, FFT_MINOR, C), kspec, 1)
    return _lmat(cs["g"], z.reshape(2 * k1p, FFT_MINOR * C), F32).reshape(L, C)


def _out_kernel(yh_ref, gate_ref, ya_ref, wh_ref, wa_ref, ng_ref, x_ref, g_ref, b_ref, o_ref):
    yh = yh_ref[...] * gate_ref[...]
    yh = yh * lax.rsqrt(jnp.mean(yh * yh, axis=-1, keepdims=True) + LN_EPS) * ng_ref[...]
    acc = jnp.dot(yh.astype(BF16), wh_ref[...], preferred_element_type=F32)
    acc = acc + jnp.dot(ya_ref[...], wa_ref[...], preferred_element_type=F32)
    o_ref[...] = _layer_norm(ALPHA * x_ref[...] + acc, g_ref[...], b_ref[...])


def _out_ln(y_hy, z, gate_block, y_at, w_out, norm_g, x1, g, b, tm=512):
    L, D = x1.shape
    ch, ca = y_hy.shape[1], y_at.shape[1]
    tm = min(tm, L)
    row = lambda c: pl.BlockSpec((tm, c), lambda i: (i, 0))
    full = lambda r, c: pl.BlockSpec((r, c), lambda i: (0, 0))
    return pl.pallas_call(
        _out_kernel,
        out_shape=jax.ShapeDtypeStruct((L, D), F32),
        grid=(L // tm,),
        in_specs=[row(ch), pl.BlockSpec((tm, ch), lambda i: (i, gate_block)), row(ca),
                  pl.BlockSpec((ch, D), lambda i: (0, 0)),
                  pl.BlockSpec((ca, D), lambda i: (1, 0)),
                  full(1, ch), row(D), full(1, D), full(1, D)],
        out_specs=row(D),
        compiler_params=_cparams(("parallel",)),
        name="out_ln",
    )(y_hy, z, y_at, w_out, w_out, norm_g, x1, g, b)


def kernel(x, ffn1_w_gate, ffn1_w_up, ffn1_w_down, ln1_g, ln1_b, w_in, hyena_conv_w, hyena_conv_b, filt_w1, filt_b1, filt_w2, filt_b2, filt_w3, filt_b3, filt_freq, filt_w_out, hyena_bias, hyena_norm_g, lambda_q1, lambda_k1, lambda_q2, lambda_k2, subln_g, w_out, ln2_g, ln2_b, ffn2_w_gate, ffn2_w_up, ffn2_w_down, ln3_g, ln3_b):
    assert x.shape[0] == 1 and ffn1_w_gate.shape[0] == DEPTH
    L = x.shape[1]
    bf = lambda a: a.astype(BF16)
    row = lambda a: a.reshape(1, -1)
    h = x[0]
    s_hy = 3 * D_HYENA
    for i in range(DEPTH):
        h = _ffn_ln(h, bf(ffn1_w_gate[i]), bf(ffn1_w_up[i]), bf(ffn1_w_down[i]), row(ln1_g[i]), row(ln1_b[i]))
        w_i = w_in[i]
        p_hy = _proj(h, bf(w_i[:, :s_hy]), BF16)
        q_t = _proj_q_t(h, bf(w_i[:, s_hy:s_hy + D_QK]))
        k_at = _proj_rope(h, bf(w_i[:, s_hy + D_QK:s_hy + 2 * D_QK]))
        v_t = _proj_v_t(h, bf(w_i[:, s_hy + 2 * D_QK:]), ATTN_TK)
        z = _short_conv(p_hy, hyena_conv_w[i], row(hyena_conv_b[i]))
        kfilt = _hyena_filters(L, filt_w1[i], filt_b1[i], filt_w2[i], filt_b2[i], filt_w3[i], filt_b3[i],
                               filt_freq[i], filt_w_out[i], hyena_bias[i])
        y_hy = _hyena_long_convs(z[:, :D_HYENA], z[:, D_HYENA:2 * D_HYENA], kfilt, L)
        y_at = _diff_attention(q_t, k_at, v_t, row(lambda_q1[i]), row(lambda_k1[i]), row(lambda_q2[i]),
                               row(lambda_k2[i]), row(subln_g[i]))
        h = _out_ln(y_hy, z, 2, y_at, bf(w_out[i]), row(hyena_norm_g[i]), h, row(ln2_g[i]), row(ln2_b[i]))
        h = _ffn_ln(h, bf(ffn2_w_gate[i]), bf(ffn2_w_up[i]), bf(ffn2_w_down[i]), row(ln3_g[i]), row(ln3_b[i]))
    return h[None]
```

```python
import functools
import math

import numpy as np
import jax
import jax.numpy as jnp
from jax import lax
from jax.experimental import pallas as pl
from jax.experimental.pallas import tpu as pltpu

F32 = jnp.float32
BF16 = jnp.bfloat16

D_HYENA = 1024
N_HEADS = 8
HEAD_DIM = 64
VAL_DIM = 128
D_QK = 2 * N_HEADS * HEAD_DIM
ROT_DIM = 16
ROPE_THETA = 500000.0
FILTER_EMB = 33
FILTER_HIDDEN = 64
DECAY_TARGET = 1e-2
MIN_DECAY = math.log(DECAY_TARGET) / 1.5
MAX_DECAY = math.log(DECAY_TARGET) / 0.3
LN_EPS = 1e-5
DEPTH = 1
ALPHA = (2.0 * DEPTH) ** 0.25
LAMBDA_INIT = 0.8 - 0.6 * math.exp(-0.3 * 0)
LOG2E = math.log2(math.e)

LANES = 128
BF16_SUBLANES = 16
FFT_MINOR = 128
VMEM_LIMIT = 56 * 1024 * 1024
ATTN_TQ = 1024
ATTN_TK = 1024
FIXED_SHIFT_MIN_DENOM = 2.0 ** -60


def _cparams(sem):
    return pltpu.CompilerParams(dimension_semantics=sem, vmem_limit_bytes=VMEM_LIMIT)


def _layer_norm(y, g, b):
    mu = jnp.mean(y, axis=-1, keepdims=True)
    d = y - mu
    var = jnp.mean(d * d, axis=-1, keepdims=True)
    return d * lax.rsqrt(var + LN_EPS) * g + b


def _ffn_kernel(x_ref, wg_ref, wu_ref, wd_ref, g_ref, b_ref, *rest, nf, with_bf16):
    if with_bf16:
        o_ref, o16_ref, xb_ref, acc_ref, h_ref = rest
    else:
        o_ref, xb_ref, acc_ref, h_ref = rest
    f = pl.program_id(1)

    def up(slot):
        xb = xb_ref[...]
        hg = jnp.dot(xb, wg_ref[...], preferred_element_type=F32)
        hu = jnp.dot(xb, wu_ref[...], preferred_element_type=F32)
        h_ref[slot] = (hg * jax.nn.sigmoid(hg) * hu).astype(BF16)

    def down(slot):
        acc_ref[...] += jnp.dot(h_ref[slot], wd_ref[...], preferred_element_type=F32)

    @pl.when(f == 0)
    def _():
        xb_ref[...] = x_ref[...].astype(BF16)
        acc_ref[...] = jnp.zeros_like(acc_ref)
        up(0)

    @pl.when(jnp.logical_and(f > 0, f < nf))
    def _():
        down((f - 1) % 2)
        up(f % 2)

    @pl.when(f == nf)
    def _():
        down((nf - 1) % 2)
        y = ALPHA * x_ref[...] + 0.5 * acc_ref[...]
        res = _layer_norm(y, g_ref[...], b_ref[...])
        o_ref[...] = res
        if with_bf16:
            o16_ref[...] = res.astype(BF16)


def _ffn_ln(x, wg, wu, wd, g, b, with_bf16=False, tm=1024, tf=512):
    L, D = x.shape
    nf = wg.shape[1] // tf
    tm = min(tm, L)
    up_w = pl.BlockSpec((D, tf), lambda i, f: (0, jnp.minimum(f, nf - 1)))
    once = dict(pipeline_mode=pl.Buffered(1))
    out_spec = pl.BlockSpec((tm, D), lambda i, f: (i, 0), **once)
    out_shape = jax.ShapeDtypeStruct((L, D), F32)
    if with_bf16:
        out_spec = (out_spec, out_spec)
        out_shape = (out_shape, jax.ShapeDtypeStruct((L, D), BF16))
    return pl.pallas_call(
        functools.partial(_ffn_kernel, nf=nf, with_bf16=with_bf16),
        out_shape=out_shape,
        grid=(L // tm, nf + 1),
        in_specs=[
            pl.BlockSpec((tm, D), lambda i, f: (i, 0), **once),
            up_w, up_w,
            pl.BlockSpec((tf, D), lambda i, f: (jnp.maximum(f - 1, 0), 0)),
            pl.BlockSpec((1, D), lambda i, f: (0, 0)),
            pl.BlockSpec((1, D), lambda i, f: (0, 0)),
        ],
        out_specs=out_spec,
        scratch_shapes=[pltpu.VMEM((tm, D), BF16), pltpu.VMEM((tm, D), F32), pltpu.VMEM((2, tm, tf), BF16)],
        compiler_params=_cparams(("parallel", "arbitrary")),
        name="ffn_ln",
    )(x, wg, wu, wd, g, b)


def _proj_kernel(x_ref, w_ref, o_ref, xb_ref):
    @pl.when(pl.program_id(1) == 0)
    def _():
        xb_ref[...] = x_ref[...].astype(BF16)

    o_ref[...] = jnp.dot(xb_ref[...], w_ref[...], preferred_element_type=F32).astype(o_ref.dtype)


def _proj(x, w, out_dtype, tm=1024, tn=1024):
    L, D = x.shape
    N = w.shape[1]
    tm = min(tm, L)
    return pl.pallas_call(
        _proj_kernel,
        out_shape=jax.ShapeDtypeStruct((L, N), out_dtype),
        grid=(L // tm, N // tn),
        in_specs=[pl.BlockSpec((tm, D), lambda i, j: (i, 0)),
                  pl.BlockSpec((D, tn), lambda i, j: (0, j))],
        out_specs=pl.BlockSpec((tm, tn), lambda i, j: (i, j)),
        scratch_shapes=[pltpu.VMEM((tm, D), BF16)],
        compiler_params=_cparams(("parallel", "arbitrary")),
        name="proj",
    )(x, w)


def _rope_tables(L, scale):
    half = ROT_DIM // 2
    inv = ROPE_THETA ** (-np.arange(0, ROT_DIM, 2, dtype=np.float64) / ROT_DIM)
    pos = np.arange(L, dtype=np.float64)[:, None]
    d = np.arange(LANES) % HEAD_DIM
    ang = pos * inv[d % half][None, :]
    c = np.where(d[None, :] < ROT_DIM, np.cos(ang), 1.0) * scale
    s1 = np.where(d[None, :] < half, -np.sin(ang), 0.0) * scale
    s2 = np.where((d[None, :] >= half) & (d[None, :] < ROT_DIM), np.sin(ang), 0.0) * scale
    return c, s1, s2


def _proj_rope_kernel(x_ref, w_ref, c_ref, s1_ref, s2_ref, o_ref, xb_ref, *, tn):
    @pl.when(pl.program_id(1) == 0)
    def _():
        xb_ref[...] = x_ref[...].astype(BF16)

    p = jnp.dot(xb_ref[...], w_ref[...], preferred_element_type=F32)
    c, s1, s2 = c_ref[...], s1_ref[...], s2_ref[...]
    shift = ROT_DIM // 2
    for h in range(tn // LANES):
        ph = p[:, h * LANES:(h + 1) * LANES]
        r = ph * c + pltpu.roll(ph, LANES - shift, 1) * s1 + pltpu.roll(ph, shift, 1) * s2
        o_ref[:, h * LANES:(h + 1) * LANES] = r.astype(o_ref.dtype)


def _proj_rope(x, w, tm=1024, tn=1024):
    L, D = x.shape
    N = w.shape[1]
    tm = min(tm, L)
    tabs = [jnp.asarray(t, F32) for t in _rope_tables(L, 1.0)]
    tab = pl.BlockSpec((tm, LANES), lambda i, j: (i, 0))
    return pl.pallas_call(
        functools.partial(_proj_rope_kernel, tn=tn),
        out_shape=jax.ShapeDtypeStruct((L, N), BF16),
        grid=(L // tm, N // tn),
        in_specs=[pl.BlockSpec((tm, D), lambda i, j: (i, 0)),
                  pl.BlockSpec((D, tn), lambda i, j: (0, j)),
                  tab, tab, tab],
        out_specs=pl.BlockSpec((tm, tn), lambda i, j: (i, j)),
        scratch_shapes=[pltpu.VMEM((tm, D), BF16)],
        compiler_params=_cparams(("parallel", "arbitrary")),
        name="proj_rope",
    )(x, w, *tabs)


def _projT_kernel(x_ref, w_ref, *rest, rope, tn, tk):
    if rope:
        c_ref, s1_ref, s2_ref, o_ref, xb_ref = rest
    else:
        o_ref, xb_ref = rest

    @pl.when(pl.program_id(1) == 0)
    def _():
        xb_ref[...] = x_ref[...].astype(BF16)

    pt = jnp.dot(xb_ref[...], w_ref[...], preferred_element_type=F32).T
    if rope:
        c, s1, s2 = c_ref[...], s1_ref[...], s2_ref[...]
        shift = ROT_DIM // 2
        for h in range(tn // LANES):
            ph = pt[h * LANES:(h + 1) * LANES, :]
            r = ph * c + pltpu.roll(ph, LANES - shift, 0) * s1 + pltpu.roll(ph, shift, 0) * s2
            o_ref[h * LANES:(h + 1) * LANES, :] = r.astype(o_ref.dtype)
    else:
        for s in range(pt.shape[1] // tk):
            o_ref[s] = pt[:, s * tk:(s + 1) * tk].astype(o_ref.dtype)


def _proj_q_t(x, w, tm=1024, tn=512):
    L, D = x.shape
    N = w.shape[1]
    tm = min(tm, L)
    tabs = [jnp.asarray(t.T, F32) for t in _rope_tables(L, HEAD_DIM ** -0.5 * LOG2E)]
    tab = pl.BlockSpec((LANES, tm), lambda i, j: (0, i))
    return pl.pallas_call(
        functools.partial(_projT_kernel, rope=True, tn=tn, tk=None),
        out_shape=jax.ShapeDtypeStruct((N, L), BF16),
        grid=(L // tm, N // tn),
        in_specs=[pl.BlockSpec((tm, D), lambda i, j: (i, 0)),
                  pl.BlockSpec((D, tn), lambda i, j: (0, j)),
                  tab, tab, tab],
        out_specs=pl.BlockSpec((tn, tm), lambda i, j: (j, i)),
        scratch_shapes=[pltpu.VMEM((tm, D), BF16)],
        compiler_params=_cparams(("parallel", "arbitrary")),
        name="proj_q_t",
    )(x, w, *tabs)


def _proj_v_t(x, w, tk, tm=1024, tn=512):
    L, D = x.shape
    N = w.shape[1]
    tm = min(tm, L)
    tk = min(tk, L)
    per = tm // tk
    return pl.pallas_call(
        functools.partial(_projT_kernel, rope=False, tn=tn, tk=tk),
        out_shape=jax.ShapeDtypeStruct((L // tk, N, tk), BF16),
        grid=(L // tm, N // tn),
        in_specs=[pl.BlockSpec((tm, D), lambda i, j: (i, 0)),
                  pl.BlockSpec((D, tn), lambda i, j: (0, j))],
        out_specs=pl.BlockSpec((per, tn, tk), lambda i, j: (i, j, 0)),
        scratch_shapes=[pltpu.VMEM((tm, D), BF16)],
        compiler_params=_cparams(("parallel", "arbitrary")),
        name="proj_v_t",
    )(x, w)


def _attn_kernel(qt_ref, k_ref, vt_ref, lq1_ref, lk1_ref, lq2_ref, lk2_ref, g_ref, o_ref,
                 acc_ref, l_ref, kn_ref, sa_ref, sb_ref, pa_ref, pb_ref, *, tk, nk):
    qt = qt_ref[...].astype(F32)
    tq = qt.shape[1]
    row = lax.broadcasted_iota(jnp.int32, qt.shape, 0)
    qs = (jnp.where(row < HEAD_DIM, qt, 0.0).astype(BF16),
          jnp.where(row >= HEAD_DIM, qt, 0.0).astype(BF16))
    zero = jnp.zeros((1, tq), F32)

    def k_chunk(j):
        return k_ref[pl.ds(pl.multiple_of(j * tk, tk), tk), :]

    @pl.when(pl.program_id(1) == 0)
    def _():
        def kbody(c, carry):
            kb = k_chunk(c).astype(F32)
            sq = kb * kb
            lane = lax.broadcasted_iota(jnp.int32, sq.shape, 1)
            n0 = jnp.sum(jnp.where(lane < HEAD_DIM, sq, 0.0), axis=1, keepdims=True)
            n1 = jnp.sum(jnp.where(lane >= HEAD_DIM, sq, 0.0), axis=1, keepdims=True)
            return (jnp.maximum(carry[0], jnp.max(n0, axis=0, keepdims=True)),
                    jnp.maximum(carry[1], jnp.max(n1, axis=0, keepdims=True)))

        z = jnp.zeros((1, 1), F32)
        k0, k1 = lax.fori_loop(0, nk, kbody, (z, z))
        kn_ref[0] = jnp.broadcast_to(k0, (1, LANES))
        kn_ref[1] = jnp.broadcast_to(k1, (1, LANES))

    q2 = qt * qt
    qn = (jnp.sum(q2[:HEAD_DIM], axis=0, keepdims=True), jnp.sum(q2[HEAD_DIM:], axis=0, keepdims=True))
    mb = tuple(jnp.sqrt(qn[mi] * kn_ref[mi][:, 0:1]) for mi in range(2))
    acc_ref[...] = jnp.zeros_like(acc_ref)

    def fixed_probs(j, p_ref):
        kb = k_chunk(j)
        sums = []
        for mi in range(2):
            pt = jnp.exp2(jnp.dot(kb, qs[mi], preferred_element_type=F32) - mb[mi])
            p_ref[mi] = pt.astype(BF16)
            sums.append(jnp.sum(pt, axis=0, keepdims=True))
        return tuple(sums)

    def fixed_accumulate(j, p_ref):
        vb = vt_ref[j]
        for mi in range(2):
            acc_ref[mi] += jnp.dot(vb, p_ref[mi], preferred_element_type=F32)

    def fixed_body(jj, carry):
        ls, la = carry
        j = 2 * jj
        lb = fixed_probs(j + 1, pb_ref)
        fixed_accumulate(j, pa_ref)
        ls = (ls[0] + la[0] + lb[0], ls[1] + la[1] + lb[1])
        la = fixed_probs(j + 2, pa_ref)
        fixed_accumulate(j + 1, pb_ref)
        return ls, la

    ls, la = lax.fori_loop(0, nk // 2 - 1, fixed_body, ((zero, zero), fixed_probs(0, pa_ref)))
    lb = fixed_probs(nk - 1, pb_ref)
    fixed_accumulate(nk - 2, pa_ref)
    fixed_accumulate(nk - 1, pb_ref)
    ls = (ls[0] + la[0] + lb[0], ls[1] + la[1] + lb[1])
    l_ref[0] = ls[0]
    l_ref[1] = ls[1]
    trusted = jnp.min(jnp.minimum(ls[0], ls[1])) >= FIXED_SHIFT_MIN_DENOM

    @pl.when(jnp.logical_not(trusted))
    def _():
        acc_ref[...] = jnp.zeros_like(acc_ref)
        neg = jnp.full((1, tq), -jnp.inf, F32)

        def scores(j, s_ref):
            kb = k_chunk(j)
            mx = []
            for mi in range(2):
                st = jnp.dot(kb, qs[mi], preferred_element_type=F32)
                s_ref[mi] = st
                mx.append(jnp.max(st, axis=0, keepdims=True))
            return tuple(mx)

        def consume(j, s_ref, mx, ms, lr):
            vb = vt_ref[j]
            new_m, new_l = [], []
            for mi in range(2):
                m_next = jnp.maximum(ms[mi], mx[mi])
                alpha = jnp.exp2(ms[mi] - m_next)
                pt = jnp.exp2(s_ref[mi] - m_next)
                new_l.append(alpha * lr[mi] + jnp.sum(pt, axis=0, keepdims=True))
                new_m.append(m_next)
                acc_ref[mi] = acc_ref[mi] * alpha + jnp.dot(vb, pt.astype(BF16), preferred_element_type=F32)
            return tuple(new_m), tuple(new_l)

        def body(jj, carry):
            ms, lr, mxa = carry
            j = 2 * jj
            mxb = scores(j + 1, sb_ref)
            ms, lr = consume(j, sa_ref, mxa, ms, lr)
            mxa = scores(jnp.minimum(j + 2, nk - 1), sa_ref)
            ms, lr = consume(j + 1, sb_ref, mxb, ms, lr)
            return ms, lr, mxa

        _, lr, _ = lax.fori_loop(0, nk // 2, body, ((neg, neg), (zero, zero), scores(0, sa_ref)))
        l_ref[0] = lr[0]
        l_ref[1] = lr[1]

    lam = (jnp.exp(jnp.sum(lq1_ref[...] * lk1_ref[...], axis=-1, keepdims=True))
           - jnp.exp(jnp.sum(lq2_ref[...] * lk2_ref[...], axis=-1, keepdims=True)) + LAMBDA_INIT)
    ot = acc_ref[0] / l_ref[0] - lam * (acc_ref[1] / l_ref[1])
    ot = ot * lax.rsqrt(jnp.mean(ot * ot, axis=0, keepdims=True) + LN_EPS)
    o_ref[...] = (ot.T * (g_ref[...] * (1.0 - LAMBDA_INIT))).astype(o_ref.dtype)


def _diff_attention(qt, k, vt, lq1, lk1, lq2, lk2, subln_g, tq=ATTN_TQ):
    L = k.shape[0]
    nk, _, tk = vt.shape
    assert nk % 2 == 0, "the kv loop handles chunks in pairs"
    tq = min(tq, L)
    vec = pl.BlockSpec((1, HEAD_DIM), lambda h, i: (0, 0))
    return pl.pallas_call(
        functools.partial(_attn_kernel, tk=tk, nk=nk),
        out_shape=jax.ShapeDtypeStruct((L, N_HEADS * VAL_DIM), BF16),
        grid=(N_HEADS, L // tq),
        in_specs=[pl.BlockSpec((LANES, tq), lambda h, i: (h, i)),
                  pl.BlockSpec((L, LANES), lambda h, i: (0, h)),
                  pl.BlockSpec((nk, VAL_DIM, tk), lambda h, i: (0, h, 0)),
                  vec, vec, vec, vec,
                  pl.BlockSpec((1, VAL_DIM), lambda h, i: (0, 0))],
        out_specs=pl.BlockSpec((tq, VAL_DIM), lambda h, i: (i, h)),
        scratch_shapes=[pltpu.VMEM((2, VAL_DIM, tq), F32), pltpu.VMEM((2, 1, tq), F32),
                        pltpu.VMEM((2, 1, LANES), F32),
                        pltpu.VMEM((2, tk, tq), F32), pltpu.VMEM((2, tk, tq), F32),
                        pltpu.VMEM((2, tk, tq), BF16), pltpu.VMEM((2, tk, tq), BF16)],
        compiler_params=_cparams(("parallel", "arbitrary")),
        name="diff_attention",
    )(qt, k, vt, lq1, lk1, lq2, lk2, subln_g)


def _conv_kernel(x_ref, w_ref, b_ref, o_ref, *, rows, nchunks):
    w0 = w_ref[0:1, :]
    w1 = w_ref[1:2, :]
    w2 = w_ref[2:3, :]
    b = b_ref[...]
    grp = BF16_SUBLANES
    last = nchunks * rows - grp

    def body(i, carry):
        r0 = pl.multiple_of(i * rows, rows)
        x0 = x_ref[pl.ds(r0, rows), :].astype(F32)
        pr = x_ref[pl.ds(pl.multiple_of(jnp.maximum(r0 - grp, 0), grp), grp), :].astype(F32)
        nx = x_ref[pl.ds(pl.multiple_of(jnp.minimum(r0 + rows, last), grp), grp), :].astype(F32)
        prev_row = jnp.where(i > 0, pr[grp - 1:grp, :], 0.0)
        next_row = jnp.where(i < nchunks - 1, nx[0:1, :], 0.0)
        row = lax.broadcasted_iota(jnp.int32, x0.shape, 0)
        xm = jnp.where(row == 0, prev_row, pltpu.roll(x0, 1, 0))
        xp = jnp.where(row == rows - 1, next_row, pltpu.roll(x0, rows - 1, 0))
        o_ref[pl.ds(r0, rows), :] = (xm * w0 + x0 * w1 + xp * w2 + b).astype(o_ref.dtype)
        return carry

    lax.fori_loop(0, nchunks, body, 0)


def _short_conv(p, w, b, tn=256, rows=512):
    L, C = p.shape
    rows = min(rows, L)
    return pl.pallas_call(
        functools.partial(_conv_kernel, rows=rows, nchunks=L // rows),
        out_shape=jax.ShapeDtypeStruct((L, C), BF16),
        grid=(C // tn,),
        in_specs=[pl.BlockSpec((L, tn), lambda j: (0, j)),
                  pl.BlockSpec((3, tn), lambda j: (0, j)),
                  pl.BlockSpec((1, tn), lambda j: (0, j))],
        out_specs=pl.BlockSpec((L, tn), lambda j: (0, j)),
        compiler_params=_cparams(("parallel",)),
        name="short_conv",
    )(p, w, b)


def _filter_kernel(z_ref, w1_ref, b1_ref, w2_ref, b2_ref, w3_ref, b3_ref, fr_ref, wf_ref, wb_ref,
                   dl_ref, bias_ref, o_ref, *, L, nh):
    hp = lax.Precision.HIGHEST
    fr = fr_ref[...]
    h = jnp.sin(fr * (jnp.dot(z_ref[...], w1_ref[...], precision=hp, preferred_element_type=F32) + b1_ref[...]))
    h = jnp.sin(fr * (jnp.dot(h, w2_ref[...], precision=hp, preferred_element_type=F32) + b2_ref[...]))
    h = jnp.sin(fr * (jnp.dot(h, w3_ref[...], precision=hp, preferred_element_type=F32) + b3_ref[...]))
    hb = h.astype(BF16)

    def out_proj(w_ref):
        return jnp.dot(hb, w_ref[...], preferred_element_type=F32)

    rb = z_ref.shape[0]
    nq = rb // nh
    c2 = dl_ref.shape[1]
    row = lax.broadcasted_iota(jnp.int32, (rb, 1), 0)
    assert nh & (nh - 1) == 0
    r = jnp.bitwise_and(row, nh - 1)
    q_idx = lax.shift_right_logical(row, nh.bit_length() - 1)
    n = FFT_MINOR * r + nq * pl.program_id(0) + q_idx
    step = 1.0 / (L - 1)
    dl = dl_ref[...]
    kf = out_proj(wf_ref) * jnp.exp(-(n.astype(F32) * step) * dl)
    kf = (kf + jnp.where(n == 0, bias_ref[...], 0.0)).astype(o_ref.dtype)
    kb = out_proj(wb_ref) * jnp.exp(-((L - n).astype(F32) * step) * dl)
    kb = jnp.where(n == 0, 0.0, kb).astype(o_ref.dtype)
    for q in range(nq):
        o_ref[0:nh, q * c2:(q + 1) * c2] = kf[q * nh:(q + 1) * nh]
        o_ref[nh:2 * nh, q * c2:(q + 1) * c2] = kb[q * nh:(q + 1) * nh]


def _filter_features(L, nq):
    half = LANES // 2
    nh = L // FFT_MINOR
    bands = (FILTER_EMB - 1) // 2
    f = np.linspace(1e-4, bands - 1, bands)
    z = np.zeros((L, LANES))
    step, q, r = np.meshgrid(np.arange(FFT_MINOR // nq), np.arange(nq), np.arange(nh), indexing="ij")
    n = (FFT_MINOR * r + nq * step + q).reshape(-1).astype(np.float64)
    for off, pos in ((0, n), (half, L - n)):
        w = 2.0 * np.pi * pos / L
        z[:, off] = pos / (L - 1)
        z[:, off + 1:off + 1 + bands] = np.cos(f[None, :] * w[:, None])
        z[:, off + 1 + bands:off + 1 + 2 * bands] = -np.sin(f[None, :] * w[:, None])
    return jnp.asarray(z, F32)


def _pad_to(a, shape):
    return jnp.pad(a, [(0, s - d) for d, s in zip(a.shape, shape)])


def _block_diag2(a):
    half = LANES // 2
    p = _pad_to(a, (half, half))
    z = jnp.zeros_like(p)
    return jnp.concatenate([jnp.concatenate([p, z], axis=1), jnp.concatenate([z, p], axis=1)], axis=0)


def _hyena_filters(L, fw1, fb1, fw2, fb2, fw3, fb3, ffreq, fw_out, bias, nq=8):
    H = LANES
    half = H // 2
    C2 = 2 * D_HYENA
    nh = L // FFT_MINOR
    rb = nq * nh
    dup = lambda v: jnp.tile(_pad_to(v[None, :], (1, half)), (1, 2))
    wo = fw_out.reshape(FILTER_HIDDEN, 2, 2, D_HYENA)
    wf = _pad_to(wo[:, :, 0, :].reshape(FILTER_HIDDEN, C2), (H, C2))
    wb = jnp.concatenate([jnp.zeros((half, C2), F32), _pad_to(wo[:, :, 1, :].reshape(FILTER_HIDDEN, C2), (half, C2))])
    deltas = np.abs(np.linspace(MIN_DECAY, MAX_DECAY, D_HYENA))
    dl = jnp.asarray(np.tile(deltas, 2)[None, :], F32)
    full = lambda shape: pl.BlockSpec(shape, lambda i: (0,) * len(shape))
    return pl.pallas_call(
        functools.partial(_filter_kernel, L=L, nh=nh),
        out_shape=jax.ShapeDtypeStruct((2 * nh, FFT_MINOR * C2), BF16),
        grid=(L // rb,),
        in_specs=[pl.BlockSpec((rb, H), lambda i: (i, 0)),
                  full((H, H)), full((1, H)), full((H, H)), full((1, H)), full((H, H)), full((1, H)),
                  full((1, H)),
                  full((H, C2)), full((H, C2)),
                  full((1, C2)), full((1, C2))],
        out_specs=pl.BlockSpec((2 * nh, nq * C2), lambda i: (0, i)),
        compiler_params=_cparams(("parallel",)),
        name="hyena_filters",
    )(_filter_features(L, nq), _block_diag2(fw1), dup(fb1), _block_diag2(fw2), dup(fb2), _block_diag2(fw3),
      dup(fb3), dup(ffreq), wf.astype(BF16), wb.astype(BF16), dl, bias.reshape(1, C2))


def _dft_constants(L):
    N = 2 * L
    N1 = N // FFT_MINOR
    nh = N1 // 2
    k1h = N1 // 2 + 1
    k1p = -(-k1h // 8) * 8
    k1 = np.arange(k1h)[:, None]

    def stage_a(nn):
        th = 2 * np.pi * np.arange(nn)[None, :] * k1 / N1
        fa = np.zeros((2 * k1p, nn))
        fa[:k1h] = np.cos(th)
        fa[k1p:k1p + k1h] = -np.sin(th)
        return fa

    n2 = np.arange(FFT_MINOR)
    m1 = np.zeros((k1p, 2 * FFT_MINOR, 2 * FFT_MINOR))
    m2 = np.zeros_like(m1)
    for a in range(k1h):
        ph = -2 * np.pi * (n2[None, :] * a / N + n2[None, :] * n2[:, None] / FFT_MINOR)
        cr, ci = np.cos(ph), np.sin(ph)
        m1[a] = np.block([[cr, -ci], [ci, cr]])
        m2[a] = np.block([[cr.T, ci.T], [-ci.T, cr.T]])
    kk = np.arange(k1h)[None, :]
    cw = np.where((kk == 0) | (kk == N1 // 2), 1.0, 2.0) / N
    th = 2 * np.pi * np.arange(nh)[:, None] * kk / N1
    g = np.zeros((nh, 2 * k1p))
    g[:, :k1h] = cw * np.cos(th)
    g[:, k1p:k1p + k1h] = -cw * np.sin(th)
    as_bf16 = lambda a: jnp.asarray(a, F32).astype(BF16)
    return dict(nh=nh, n1=N1, k1h=k1h, k1p=k1p, fa_data=as_bf16(stage_a(nh)), fa_filt=as_bf16(stage_a(N1)),
                m1=as_bf16(m1), m2=as_bf16(m2), g=as_bf16(g))


def _lmat_kernel(f_ref, x_ref, o_ref):
    o_ref[...] = jnp.dot(f_ref[...], x_ref[...].astype(BF16), preferred_element_type=F32).astype(o_ref.dtype)


def _lmat(fmat, x2d, out_dtype=BF16, tn=16384):
    M, K = fmat.shape
    ncols = x2d.shape[1]
    tn = min(tn, ncols)
    return pl.pallas_call(
        _lmat_kernel,
        out_shape=jax.ShapeDtypeStruct((M, ncols), out_dtype),
        grid=(ncols // tn,),
        in_specs=[pl.BlockSpec((M, K), lambda j: (0, 0)),
                  pl.BlockSpec((K, tn), lambda j: (0, j))],
        out_specs=pl.BlockSpec((M, tn), lambda j: (0, j)),
        compiler_params=_cparams(("parallel",)),
        name="dft_stage_a",
    )(fmat, x2d)


def _lmat_gate_kernel(g_ref, z_ref, gate_ref, o_ref):
    o_ref[...] = jnp.dot(g_ref[...], z_ref[...], preferred_element_type=F32) * gate_ref[...]


def _lmat_gate(gmat, z2d, gate2d, tn=16384):
    M, K = gmat.shape
    ncols = z2d.shape[1]
    tn = min(tn, ncols)
    return pl.pallas_call(
        _lmat_gate_kernel,
        out_shape=jax.ShapeDtypeStruct((M, ncols), F32),
        grid=(ncols // tn,),
        in_specs=[pl.BlockSpec((M, K), lambda j: (0, 0)),
                  pl.BlockSpec((K, tn), lambda j: (0, j)),
                  pl.BlockSpec((M, tn), lambda j: (0, j))],
        out_specs=pl.BlockSpec((M, tn), lambda j: (0, j)),
        compiler_params=_cparams(("parallel",)),
        name="dft_stage_a_inv",
    )(gmat, z2d, gate2d)


K1_PER_STEP = 8


def _filter_spectrum_kernel(m1_ref, a_ref, o_ref):
    for b in range(m1_ref.shape[0]):
        x = jnp.concatenate([a_ref[0, b], a_ref[1, b]], axis=0)
        o_ref[b] = jnp.dot(m1_ref[b], x, preferred_element_type=F32).astype(o_ref.dtype)


def _filter_spectrum(m1, a4):
    _, k1p, n2, C = a4.shape
    kb = K1_PER_STEP
    return pl.pallas_call(
        _filter_spectrum_kernel,
        out_shape=jax.ShapeDtypeStruct((k1p, 2 * n2, C), BF16),
        grid=(k1p // kb,),
        in_specs=[pl.BlockSpec((kb, 2 * n2, 2 * n2), lambda a: (a, 0, 0)),
                  pl.BlockSpec((2, kb, n2, C), lambda a: (0, a, 0, 0))],
        out_specs=pl.BlockSpec((kb, 2 * n2, C), lambda a: (a, 0, 0)),
        compiler_params=_cparams(("parallel",)),
        name="filter_spectrum",
    )(m1, a4)


def _middle_kernel(m1_ref, m2_ref, a_ref, kf_ref, o_ref):
    n2 = a_ref.shape[2]
    for b in range(m1_ref.shape[0]):
        x = jnp.concatenate([a_ref[0, b], a_ref[1, b]], axis=0)
        s = jnp.dot(m1_ref[b], x, preferred_element_type=F32)
        sr, si = s[:n2], s[n2:]
        kf = kf_ref[b].astype(F32)
        kr, ki = kf[:n2], kf[n2:]
        y = jnp.concatenate([sr * kr - si * ki, sr * ki + si * kr], axis=0).astype(BF16)
        z = jnp.dot(m2_ref[b], y, preferred_element_type=F32).astype(o_ref.dtype)
        o_ref[0, b] = z[:n2]
        o_ref[1, b] = z[n2:]


def _middle(m1, m2, a4, kf, order):
    _, k1p, n2, C = a4.shape
    kb = K1_PER_STEP
    mat = pl.BlockSpec((kb, 2 * n2, 2 * n2), lambda a: (a, 0, 0))
    return pl.pallas_call(
        _middle_kernel,
        out_shape=jax.ShapeDtypeStruct((2, k1p, n2, C), BF16),
        grid=(k1p // kb,),
        in_specs=[mat, mat,
                  pl.BlockSpec((2, kb, n2, C), lambda a: (0, a, 0, 0)),
                  pl.BlockSpec((kb, 2 * n2, C), lambda a: (a, 0, order))],
        out_specs=pl.BlockSpec((2, kb, n2, C), lambda a: (0, a, 0, 0)),
        compiler_params=_cparams(("parallel",)),
        name="dft_middle",
    )(m1, m2, a4, kf)


def _hyena_long_convs(v, g1, kfilt2d, L):
    C = v.shape[1]
    cs = _dft_constants(L)
    nh, k1p = cs["nh"], cs["k1p"]
    fa = _lmat(cs["fa_filt"], kfilt2d)
    kspec = _filter_spectrum(cs["m1"], fa.reshape(2, k1p, FFT_MINOR, 2 * C))
    a = _lmat(cs["fa_data"], v.reshape(nh, FFT_MINOR * C))
    z = _middle(cs["m1"], cs["m2"], a.reshape(2, k1p, FFT_MINOR, C), kspec, 0)
    y2d = _lmat_gate(cs["g"], z.reshape(2 * k1p, FFT_MINOR * C), g1.reshape(nh, FFT_MINOR * C))
    a = _lmat(cs["fa_data"], y2d)
    z = _middle(cs["m1"], cs["m2"], a.reshape(2, k1p, FFT_MINOR, C), kspec, 1)
    return _lmat(cs["g"], z.reshape(2 * k1p, FFT_MINOR * C), F32).reshape(L, C)


def _out_kernel(yh_ref, gate_ref, ya_ref, wh_ref, wa_ref, ng_ref, x_ref, g_ref, b_ref, o_ref):
    yh = yh_ref[...] * gate_ref[...]
    yh = yh * lax.rsqrt(jnp.mean(yh * yh, axis=-1, keepdims=True) + LN_EPS) * ng_ref[...]
    acc = jnp.dot(yh.astype(BF16), wh_ref[...], preferred_element_type=F32)
    acc = acc + jnp.dot(ya_ref[...], wa_ref[...], preferred_element_type=F32)
    o_ref[...] = _layer_norm(ALPHA * x_ref[...] + acc, g_ref[...], b_ref[...])


def _out_ln(y_hy, z, gate_block, y_at, w_out, norm_g, x1, g, b, tm=512):
    L, D = x1.shape
    ch, ca = y_hy.shape[1], y_at.shape[1]
    tm = min(tm, L)
    row = lambda c: pl.BlockSpec((tm, c), lambda i: (i, 0))
    full = lambda r, c: pl.BlockSpec((r, c), lambda i: (0, 0))
    return pl.pallas_call(
        _out_kernel,
        out_shape=jax.ShapeDtypeStruct((L, D), F32),
        grid=(L // tm,),
        in_specs=[row(ch), pl.BlockSpec((tm, ch), lambda i: (i, gate_block)), row(ca),
                  pl.BlockSpec((ch, D), lambda i: (0, 0)),
                  pl.BlockSpec((ca, D), lambda i: (1, 0)),
                  full(1, ch), row(D), full(1, D), full(1, D)],
        out_specs=row(D),
        compiler_params=_cparams(("parallel",)),
        name="out_ln",
    )(y_hy, z, y_at, w_out, w_out, norm_g, x1, g, b)


def kernel(x, ffn1_w_gate, ffn1_w_up, ffn1_w_down, ln1_g, ln1_b, w_in, hyena_conv_w, hyena_conv_b, filt_w1, filt_b1, filt_w2, filt_b2, filt_w3, filt_b3, filt_freq, filt_w_out, hyena_bias, hyena_norm_g, lambda_q1, lambda_k1, lambda_q2, lambda_k2, subln_g, w_out, ln2_g, ln2_b, ffn2_w_gate, ffn2_w_up, ffn2_w_down, ln3_g, ln3_b):
    assert x.shape[0] == 1 and ffn1_w_gate.shape[0] == DEPTH
    L = x.shape[1]
    bf = lambda a: a.astype(BF16)
    row = lambda a: a.reshape(1, -1)
    h = x[0]
    s_hy = 3 * D_HYENA
    for i in range(DEPTH):
        h, hb = _ffn_ln(h, bf(ffn1_w_gate[i]), bf(ffn1_w_up[i]), bf(ffn1_w_down[i]), row(ln1_g[i]), row(ln1_b[i]),
                        with_bf16=True)
        w_i = w_in[i]
        p_hy = _proj(hb, bf(w_i[:, :s_hy]), BF16)
        q_t = _proj_q_t(hb, bf(w_i[:, s_hy:s_hy + D_QK]))
        k_at = _proj_rope(hb, bf(w_i[:, s_hy + D_QK:s_hy + 2 * D_QK]))
        v_t = _proj_v_t(hb, bf(w_i[:, s_hy + 2 * D_QK:]), ATTN_TK)
        z = _short_conv(p_hy, hyena_conv_w[i], row(hyena_conv_b[i]))
        kfilt = _hyena_filters(L, filt_w1[i], filt_b1[i], filt_w2[i], filt_b2[i], filt_w3[i], filt_b3[i],
                               filt_freq[i], filt_w_out[i], hyena_bias[i])
        y_hy = _hyena_long_convs(z[:, :D_HYENA], z[:, D_HYENA:2 * D_HYENA], kfilt, L)
        y_at = _diff_attention(q_t, k_at, v_t, row(lambda_q1[i]), row(lambda_k1[i]), row(lambda_q2[i]),
                               row(lambda_k2[i]), row(subln_g[i]))
        h = _out_ln(y_hy, z, 2, y_at, bf(w_out[i]), row(hyena_norm_g[i]), h, row(ln2_g[i]), row(ln2_b[i]))
        h = _ffn_ln(h, bf(ffn2_w_gate[i]), bf(ffn2_w_up[i]), bf(ffn2_w_down[i]), row(ln3_g[i]), row(ln3_b[i]))
    return h[None]
```
